```python
import math, functools
import jax, jax.numpy as jnp
from jax import lax
import numpy as np

D_MODEL = 1024
BATCH = 8
SEQ = 2048
DEPTH = 1
DEC_BATCH = 128
DEC_SEQ = 1
PAST_LEN = 16384
PAGE_SIZE = 128

N_META = 16
H_A = 4
D_HEAD_A = 64
D_V_A = 2 * D_HEAD_A
H_B = 8
Q_LORA = 384
KV_LORA = 256
D_NOPE = 64
D_ROPE = 32
D_V_B = 64
ROPE_BASE = 10000.0
N_BUCKETS = 32
MAX_DISTANCE = 128
D_FF = 2816
CONV_WIDTH = 3
Q_BLOCK = 128
EPS = 1e-6
NEG = -1e30
SCALE_A = D_HEAD_A ** -0.5
SCALE_B = (D_NOPE + D_ROPE) ** -0.5
W_QA = H_A * 2 * D_HEAD_A
W_KA = H_A * 2 * D_HEAD_A
W_VA = H_A * D_V_A
D_IN = W_QA + W_KA + W_VA + Q_LORA + KV_LORA + D_ROPE
SPLITS = (W_QA, W_QA + W_KA, W_QA + W_KA + W_VA, W_QA + W_KA + W_VA + Q_LORA,
          W_QA + W_KA + W_VA + Q_LORA + KV_LORA)

kernel_name = 'hybrid_diffattn_mla_convffn_step'


def rms_norm(x, w):
    xf = x.astype(jnp.float32)
    y = xf * lax.rsqrt(jnp.mean(xf * xf, axis=-1, keepdims=True) + EPS)
    return (y * w.astype(jnp.float32)).astype(x.dtype)


def lambda_init(layer):
    return 0.8 - 0.6 * math.exp(-0.3 * layer)


def t5_bias(rel_bias, q_pos, k_pos):
    n = jnp.maximum(q_pos[:, None] - k_pos[None, :], 0)
    max_exact = N_BUCKETS // 2
    nf = jnp.maximum(n, 1).astype(jnp.float32)
    large = max_exact + (jnp.log(nf / max_exact) / math.log(MAX_DISTANCE / max_exact)
                         * (N_BUCKETS - max_exact)).astype(jnp.int32)
    bucket = jnp.where(n < max_exact, n, jnp.minimum(large, N_BUCKETS - 1))
    return jnp.moveaxis(rel_bias[bucket].astype(jnp.float32), -1, 0)


def rope(x, pos):
    half = x.shape[-1] // 2
    freqs = ROPE_BASE ** (-jnp.arange(half, dtype=jnp.float32) / half)
    ang = pos.astype(jnp.float32)[:, None] * freqs
    ang = ang.reshape(ang.shape[:1] + (1,) * (x.ndim - 3) + (half,))
    cos, sin = jnp.cos(ang), jnp.sin(ang)
    xf = x.astype(jnp.float32)
    x1, x2 = xf[..., :half], xf[..., half:]
    return jnp.concatenate([x1 * cos - x2 * sin, x1 * sin + x2 * cos], axis=-1).astype(x.dtype)


def mixer_project(n, pos, w_in, q_norm_w, kv_norm_w, w_uq, w_ukv):
    B, T, _ = n.shape
    h = n @ w_in
    qa, ka, va, ql, kvl, kpe = jnp.split(h, SPLITS, axis=-1)
    qa = qa.reshape(B, T, H_A, 2, D_HEAD_A)
    ka = ka.reshape(B, T, H_A, 2 * D_HEAD_A)
    va = va.reshape(B, T, H_A, D_V_A)
    q = (rms_norm(ql, q_norm_w) @ w_uq).reshape(B, T, H_B, D_NOPE + D_ROPE)
    q_lat = jnp.einsum('bthn,rhn->bthr', q[..., :D_NOPE], w_ukv[..., :D_NOPE])
    q_pe = rope(q[..., D_NOPE:], pos)
    ckv = rms_norm(kvl, kv_norm_w)
    kpe = rope(kpe, pos)
    return qa, ka, va, q_lat, q_pe, ckv, kpe


def diff_scores(qa, ka, bias):
    k = ka.reshape(ka.shape[:3] + (2, D_HEAD_A))
    s = jnp.einsum('bqhmd,bkhmd->bhmqk', qa, k, preferred_element_type=jnp.float32)
    return s * SCALE_A + bias[None, :, None]


def diff_values(p, va):
    return jnp.einsum('bhmqk,bkhe->bhmqe', p, va.astype(jnp.float32))


def mla_scores(q_lat, q_pe, ckv, kpe):
    s = (jnp.einsum('bqhr,bkr->bhqk', q_lat, ckv, preferred_element_type=jnp.float32)
         + jnp.einsum('bqhp,bkp->bhqk', q_pe, kpe, preferred_element_type=jnp.float32))
    return s * SCALE_B


def mla_values(p, ckv):
    return jnp.einsum('bhqk,bkr->bhqr', p, ckv.astype(jnp.float32))


def online_update(carry, s, value_fn):
    m, l, acc = carry
    m_new = jnp.maximum(m, jnp.max(s, axis=-1))
    p = jnp.exp(s - m_new[..., None])
    corr = jnp.exp(m - m_new)
    return (m_new, l * corr + jnp.sum(p, axis=-1), acc * corr[..., None] + value_fn(p))


def diff_lambda(lq1, lk1, lq2, lk2, lam_init):
    f = lambda a: a.astype(jnp.float32)
    return jnp.exp(jnp.sum(f(lq1) * f(lk1))) - jnp.exp(jnp.sum(f(lq2) * f(lk2))) + lam_init


def diff_finish(o, lam, lam_init, subln_w, dtype):
    o = jnp.moveaxis(o[:, :, 0] - lam * o[:, :, 1], 1, 2)
    o = rms_norm(o, subln_w) * (1.0 - lam_init)
    return o.reshape(o.shape[:2] + (H_A * D_V_A,)).astype(dtype)


def mla_finish(o, w_ukv, dtype):
    out = jnp.einsum('bhqr,rhv->bqhv', o, w_ukv[..., D_NOPE:].astype(jnp.float32))
    return out.reshape(out.shape[:2] + (H_B * D_V_B,)).astype(dtype)


def prompt_attention(qa, ka, va, q_lat, q_pe, ckv, kpe, pos, rel_bias):
    def block(args):
        qa_b, ql_b, qp_b, qpos = args
        causal = qpos[:, None] >= pos[None, :]
        sa = jnp.where(causal, diff_scores(qa_b, ka, t5_bias(rel_bias, qpos, pos)), NEG)
        oa = diff_values(jax.nn.softmax(sa, axis=-1), va)
        sb = jnp.where(causal, mla_scores(ql_b, qp_b, ckv, kpe), NEG)
        ob = mla_values(jax.nn.softmax(sb, axis=-1), ckv)
        return oa, ob

    oa_m, ob_m = block((qa[:, :N_META], q_lat[:, :N_META], q_pe[:, :N_META], pos[:N_META]))
    nb = (qa.shape[1] - N_META) // Q_BLOCK

    def to_blocks(t):
        t = t[:, N_META:]
        return jnp.moveaxis(t.reshape((t.shape[0], nb, Q_BLOCK) + t.shape[2:]), 1, 0)

    oa_r, ob_r = lax.map(block, (to_blocks(qa), to_blocks(q_lat), to_blocks(q_pe),
                                 pos[N_META:].reshape(nb, Q_BLOCK)))
    oa_r = jnp.moveaxis(oa_r, 0, 3).reshape(oa_m.shape[:3] + (nb * Q_BLOCK, D_V_A))
    ob_r = jnp.moveaxis(ob_r, 0, 2).reshape(ob_m.shape[:2] + (nb * Q_BLOCK, KV_LORA))
    return jnp.concatenate([oa_m, oa_r], axis=3), jnp.concatenate([ob_m, ob_r], axis=2)


def sample_attention(qa, ka, va, q_lat, q_pe, ckv, kpe, pos, rel_bias, cache_dk, cache_dv,
                     cache_ckv, cache_kpe, page_table, layer):
    B, Tq = qa.shape[:2]
    f32 = jnp.float32
    carry_a = (jnp.full((B, H_A, 2, Tq), NEG, f32), jnp.zeros((B, H_A, 2, Tq), f32),
               jnp.zeros((B, H_A, 2, Tq, D_V_A), f32))
    carry_b = (jnp.full((B, H_B, Tq), NEG, f32), jnp.zeros((B, H_B, Tq), f32),
               jnp.zeros((B, H_B, Tq, KV_LORA), f32))

    def page_step(carry, xs):
        ca, cb = carry
        pages, p_idx = xs
        k_pos = p_idx * PAGE_SIZE + jnp.arange(PAGE_SIZE, dtype=jnp.int32)
        va_p = cache_dv[layer, pages]
        ckv_p = cache_ckv[layer, pages]
        ca = online_update(ca, diff_scores(qa, cache_dk[layer, pages], t5_bias(rel_bias, pos, k_pos)),
                           lambda p: diff_values(p, va_p))
        cb = online_update(cb, mla_scores(q_lat, q_pe, ckv_p, cache_kpe[layer, pages]),
                           lambda p: mla_values(p, ckv_p))
        return (ca, cb), None

    n_pages = page_table.shape[1]
    (ca, cb), _ = lax.scan(page_step, (carry_a, carry_b),
                           (page_table.T, jnp.arange(n_pages, dtype=jnp.int32)))
    causal = pos[:, None] >= pos[None, :]
    ca = online_update(ca, jnp.where(causal, diff_scores(qa, ka, t5_bias(rel_bias, pos, pos)), NEG),
                       lambda p: diff_values(p, va))
    cb = online_update(cb, jnp.where(causal, mla_scores(q_lat, q_pe, ckv, kpe), NEG),
                       lambda p: mla_values(p, ckv))
    return ca[2] / ca[1][..., None], cb[2] / cb[1][..., None]


def merge_branches(n, ya, yb, w_gate, w_oa, w_ob, w_o):
    g = jax.nn.sigmoid((n @ w_gate).astype(jnp.float32))
    merged = (g[..., :D_MODEL] * (ya @ w_oa).astype(jnp.float32)
              + g[..., D_MODEL:] * (yb @ w_ob).astype(jnp.float32))
    return merged.astype(n.dtype) @ w_o


def conv_ffn(n, conv_prev, w_up, conv_w, conv_b, w_down):
    T = n.shape[1]
    up = n @ w_up
    a, b = up[..., :D_FF], up[..., D_FF:]
    ext = jnp.concatenate([conv_prev.astype(a.dtype), a], axis=1)
    conv = conv_b + conv_w[0] * ext[:, 0:T]
    for j in range(1, CONV_WIDTH):
        conv = conv + conv_w[j] * ext[:, j:j + T]
    h = jax.nn.gelu(conv) * b
    return h @ w_down, ext[:, T:]


def layer_forward(x, pos, lw, lam_init, attend, conv_prev):
    n = rms_norm(x, lw['norm_mix_pre'])
    qa, ka, va, q_lat, q_pe, ckv, kpe = mixer_project(n, pos, lw['w_in'], lw['q_norm_w'],
                                                      lw['kv_norm_w'], lw['w_uq'], lw['w_ukv'])
    oa, ob = attend(qa, ka, va, q_lat, q_pe, ckv, kpe)
    lam = diff_lambda(lw['lambda_q1'], lw['lambda_k1'], lw['lambda_q2'], lw['lambda_k2'], lam_init)
    ya = diff_finish(oa, lam, lam_init, lw['subln_w'], x.dtype)
    yb = mla_finish(ob, lw['w_ukv'], x.dtype)
    mix = merge_branches(n, ya, yb, lw['w_gate'], lw['w_oa'], lw['w_ob'], lw['w_o'])
    h = x + rms_norm(mix, lw['norm_mix_post'])
    f, conv_new = conv_ffn(rms_norm(h, lw['norm_ffn_pre']), conv_prev, lw['w_up'], lw['conv_w'],
                           lw['conv_b'], lw['w_down'])
    return h + rms_norm(f, lw['norm_ffn_post']), ka, va, ckv, kpe, conv_new


def setup_inputs(seed: int = 0) -> dict:
    key = jax.random.key(seed)
    keys = jax.random.split(key, 40)
    f32 = jnp.float32
    L = DEPTH

    def nrm(i, shape, scale):
        return jax.random.normal(keys[i], shape, f32) * scale

    n_pages = PAST_LEN // PAGE_SIZE
    n_used = DEC_BATCH * n_pages
    n_pool = n_used + max(1, n_used // 4)
    page_table = jax.random.permutation(keys[0], n_pool)[:n_used].reshape(DEC_BATCH, n_pages).astype(jnp.int32)
    return {
        'x_prompt': nrm(1, (BATCH, SEQ, D_MODEL), 1.0),
        'x_sample': nrm(2, (DEC_BATCH, DEC_SEQ, D_MODEL), 1.0),
        'cache_diff_k': nrm(3, (L, n_pool, PAGE_SIZE, H_A, 2 * D_HEAD_A), 1.0),
        'cache_diff_v': nrm(4, (L, n_pool, PAGE_SIZE, H_A, D_V_A), 1.0),
        'cache_mla_latent': nrm(5, (L, n_pool, PAGE_SIZE, KV_LORA), 1.0),
        'cache_mla_rope': nrm(6, (L, n_pool, PAGE_SIZE, D_ROPE), 1.0),
        'state_conv': nrm(7, (L, DEC_BATCH, CONV_WIDTH - 1, D_FF), 1.0),
        'page_table': page_table,
        'meta_tokens': nrm(8, (N_META, D_MODEL), 1.0),
        'rel_bias': nrm(9, (N_BUCKETS, H_A), 0.5),
        'norm_mix_pre': 1.0 + nrm(10, (L, D_MODEL), 0.02),
        'norm_mix_post': 1.0 + nrm(11, (L, D_MODEL), 0.02),
        'norm_ffn_pre': 1.0 + nrm(12, (L, D_MODEL), 0.02),
        'norm_ffn_post': 1.0 + nrm(13, (L, D_MODEL), 0.02),
        'w_in': nrm(14, (L, D_MODEL, D_IN), D_MODEL ** -0.5),
        'w_gate': nrm(15, (L, D_MODEL, 2 * D_MODEL), D_MODEL ** -0.5),
        'lambda_q1': nrm(16, (L, D_HEAD_A), 0.1),
        'lambda_k1': nrm(17, (L, D_HEAD_A), 0.1),
        'lambda_q2': nrm(18, (L, D_HEAD_A), 0.1),
        'lambda_k2': nrm(19, (L, D_HEAD_A), 0.1),
        'subln_w': 1.0 + nrm(20, (L, D_V_A), 0.02),
        'q_norm_w': 1.0 + nrm(21, (L, Q_LORA), 0.02),
        'kv_norm_w': 1.0 + nrm(22, (L, KV_LORA), 0.02),
        'w_uq': nrm(23, (L, Q_LORA, H_B * (D_NOPE + D_ROPE)), Q_LORA ** -0.5),
        'w_ukv': nrm(24, (L, KV_LORA, H_B, D_NOPE + D_V_B), KV_LORA ** -0.5),
        'w_oa': nrm(25, (L, H_A * D_V_A, D_MODEL), (H_A * D_V_A) ** -0.5),
        'w_ob': nrm(26, (L, H_B * D_V_B, D_MODEL), (H_B * D_V_B) ** -0.5),
        'w_o': nrm(27, (L, D_MODEL, D_MODEL), D_MODEL ** -0.5),
        'w_up': nrm(28, (L, D_MODEL, 2 * D_FF), D_MODEL ** -0.5),
        'conv_w': nrm(29, (L, CONV_WIDTH, D_FF), CONV_WIDTH ** -0.5),
        'conv_b': nrm(30, (L, D_FF), 0.01),
        'w_down': nrm(31, (L, D_FF, D_MODEL), D_FF ** -0.5),
    }


def reference(x_prompt, x_sample, cache_diff_k, cache_diff_v, cache_mla_latent, cache_mla_rope,
              state_conv, page_table, meta_tokens, rel_bias, norm_mix_pre, norm_mix_post,
              norm_ffn_pre, norm_ffn_post, w_in, w_gate, lambda_q1, lambda_k1, lambda_q2, lambda_k2,
              subln_w, q_norm_w, kv_norm_w, w_uq, w_ukv, w_oa, w_ob, w_o, w_up, conv_w, conv_b,
              w_down):
    B = x_prompt.shape[0]
    meta = jnp.broadcast_to(meta_tokens.astype(x_prompt.dtype)[None], (B, N_META, D_MODEL))
    hp = jnp.concatenate([meta, x_prompt], axis=1)
    pos_p = jnp.arange(hp.shape[1], dtype=jnp.int32)
    hs = x_sample
    pos_s = PAST_LEN + jnp.arange(x_sample.shape[1], dtype=jnp.int32)
    zeros_conv = jnp.zeros((B, CONV_WIDTH - 1, D_FF), hp.dtype)

    pk, pv, pc, pr, pcv = [], [], [], [], []
    sk, sv, sc, sr, scv = [], [], [], [], []
    for l in range(DEPTH):
        lw = dict(norm_mix_pre=norm_mix_pre[l], norm_mix_post=norm_mix_post[l],
                  norm_ffn_pre=norm_ffn_pre[l], norm_ffn_post=norm_ffn_post[l], w_in=w_in[l],
                  w_gate=w_gate[l], lambda_q1=lambda_q1[l], lambda_k1=lambda_k1[l],
                  lambda_q2=lambda_q2[l], lambda_k2=lambda_k2[l], subln_w=subln_w[l],
                  q_norm_w=q_norm_w[l], kv_norm_w=kv_norm_w[l], w_uq=w_uq[l], w_ukv=w_ukv[l],
                  w_oa=w_oa[l], w_ob=w_ob[l], w_o=w_o[l], w_up=w_up[l], conv_w=conv_w[l],
                  conv_b=conv_b[l], w_down=w_down[l])
        lam0 = lambda_init(l)
        prompt_attend = functools.partial(prompt_attention, pos=pos_p, rel_bias=rel_bias)
        sample_attend = functools.partial(sample_attention, pos=pos_s, rel_bias=rel_bias,
                                          cache_dk=cache_diff_k, cache_dv=cache_diff_v,
                                          cache_ckv=cache_mla_latent, cache_kpe=cache_mla_rope,
                                          page_table=page_table, layer=l)
        hp, k_, v_, c_, r_, cv_ = layer_forward(hp, pos_p, lw, lam0, prompt_attend, zeros_conv)
        pk.append(k_); pv.append(v_); pc.append(c_); pr.append(r_); pcv.append(cv_)
        hs, k_, v_, c_, r_, cv_ = layer_forward(hs, pos_s, lw, lam0, sample_attend, state_conv[l])
        sk.append(k_); sv.append(v_); sc.append(c_); sr.append(r_); scv.append(cv_)

    return (hp[:, N_META:], hs, jnp.stack(pk), jnp.stack(pv), jnp.stack(pc), jnp.stack(pr),
            jnp.stack(pcv), jnp.stack(sk), jnp.stack(sv), jnp.stack(sc), jnp.stack(sr),
            jnp.stack(scv))
```

```python
import functools
import math

import numpy as np
import jax
import jax.numpy as jnp
from jax import lax
from jax.experimental import pallas as pl
from jax.experimental.pallas import tpu as pltpu

F32 = jnp.float32
BF16 = jnp.bfloat16

D_MODEL = 1024
BATCH = 8
SEQ = 2048
DEC_BATCH = 128
PAST_LEN = 16384
PAGE_SIZE = 128
N_PAGES = PAST_LEN // PAGE_SIZE
N_META = 16
H_A = 4
D_HEAD_A = 64
D_V_A = 128
H_B = 8
Q_LORA = 384
KV_LORA = 256
D_NOPE = 64
D_ROPE = 32
D_V_B = 64
ROPE_BASE = 10000.0
N_BUCKETS = 32
MAX_DISTANCE = 128
D_FF = 2816
EPS = 1e-6
NEG = -1e30
SCALE_A = D_HEAD_A ** -0.5
SCALE_B = (D_NOPE + D_ROPE) ** -0.5
LAM_INIT = 0.8 - 0.6 * math.exp(-0.3 * 0)

T_P = SEQ + N_META
LANES = 128
T_PAD = 17 * LANES
TM_DENSE = T_PAD // 8
TQ = LANES
PAGES_PER_STEP = 8
VMEM_LIMIT = 56 * 1024 * 1024

W_A = H_A * 2 * D_HEAD_A
IN_QL = 3 * W_A
IN_KVL = IN_QL + Q_LORA
IN_KPE = IN_KVL + KV_LORA
IN_COLS = IN_KPE + LANES
Q_NOPE_COLS = H_B * D_NOPE
Q_ROPE_COLS = H_B * D_ROPE


def _rms(x, w):
    return x * lax.rsqrt(jnp.mean(x * x, axis=-1, keepdims=True) + EPS) * w


def _rope_lanes(x, c, s):
    lane = lax.broadcasted_iota(jnp.int32, x.shape, 1)
    from_right = pltpu.roll(x, LANES - 16, 1)
    from_left = pltpu.roll(x, 16, 1)
    swapped = jnp.where((lane & 16) == 0, from_right, from_left)
    return x * c + swapped * s


def _nt_dot(a, b):
    return lax.dot_general(a, b, (((1,), (1,)), ((), ())), preferred_element_type=F32)


def _dot(a, b):
    return jnp.dot(a, b, preferred_element_type=F32)


def _proj_kernel(x_ref, cos_ref, sin_ref, nw_ref, win_ref, qnw_ref, kvnw_ref, wuq_ref, wuk_ref,
                 qa_ref, qlat_ref, qpe_ref, ka_ref, va_ref, ckv_ref, kpe_ref,
                 kab_ref, vab_ref, ckvb_ref, kpeb_ref):
    x = x_ref[...]
    n = _rms(x, nw_ref[...]).astype(BF16)
    h = _dot(n, win_ref[...])
    qa_ref[...] = h[:, 0:W_A].astype(qa_ref.dtype)
    ka = h[:, W_A:2 * W_A]
    va = h[:, 2 * W_A:3 * W_A]
    ka_ref[...] = ka
    va_ref[...] = va
    kab_ref[...] = ka.astype(BF16)
    vab_ref[...] = va.astype(BF16)
    c = cos_ref[...]
    s = sin_ref[...]

    qn = _rms(h[:, IN_QL:IN_KVL], qnw_ref[...]).astype(BF16)
    q = _dot(qn, wuq_ref[...])
    for half in range(2):
        lo = Q_NOPE_COLS + half * LANES
        qpe_ref[:, half * LANES:(half + 1) * LANES] = _rope_lanes(
            q[:, lo:lo + LANES], c[:, half * LANES:(half + 1) * LANES],
            s[:, half * LANES:(half + 1) * LANES]).astype(qpe_ref.dtype)
    for pair in range(H_B // 2):
        qp = q[:, pair * LANES:(pair + 1) * LANES].astype(BF16)
        for head in (2 * pair, 2 * pair + 1):
            qlat_ref[head] = _dot(qp, wuk_ref[head]).astype(qlat_ref.dtype)

    ckv = _rms(h[:, IN_KVL:IN_KPE], kvnw_ref[...])
    ckv_ref[...] = ckv
    ckvb_ref[...] = ckv.astype(BF16)
    kpe4 = _rope_lanes(h[:, IN_KPE:IN_COLS], c[:, 0:LANES], s[:, 0:LANES])
    kpe_ref[...] = kpe4[:, 0:D_ROPE]
    kpeb_ref[...] = kpe4.astype(BF16)


def _project(x3, cos_t, sin_t, wts, tm, q_dtype):
    nb, t, _ = x3.shape
    nt = t // tm
    rows = lambda width: pl.BlockSpec((None, tm, width), lambda b, i: (b, i, 0))
    table = pl.BlockSpec((tm, Q_ROPE_COLS), lambda b, i: (i, 0))
    full = lambda a: pl.BlockSpec(a.shape, lambda b, i: (0,) * a.ndim)
    weights = (wts['norm_mix_pre'], wts['w_in'], wts['q_norm_w'], wts['kv_norm_w'], wts['w_uq'],
               wts['w_uk'])
    out_shape = (
        jax.ShapeDtypeStruct((nb, t, W_A), q_dtype),
        jax.ShapeDtypeStruct((H_B, nb, t, KV_LORA), q_dtype),
        jax.ShapeDtypeStruct((nb, t, Q_ROPE_COLS), q_dtype),
        jax.ShapeDtypeStruct((nb, t, W_A), F32),
        jax.ShapeDtypeStruct((nb, t, W_A), F32),
        jax.ShapeDtypeStruct((nb, t, KV_LORA), F32),
        jax.ShapeDtypeStruct((nb, t, D_ROPE), F32),
        jax.ShapeDtypeStruct((nb, t, W_A), BF16),
        jax.ShapeDtypeStruct((nb, t, W_A), BF16),
        jax.ShapeDtypeStruct((nb, t, KV_LORA), BF16),
        jax.ShapeDtypeStruct((nb, t, LANES), BF16),
    )
    out_specs = (
        rows(W_A),
        pl.BlockSpec((H_B, None, tm, KV_LORA), lambda b, i: (0, b, i, 0)),
        rows(Q_ROPE_COLS), rows(W_A), rows(W_A), rows(KV_LORA), rows(D_ROPE),
        rows(W_A), rows(W_A), rows(KV_LORA), rows(LANES),
    )
    return pl.pallas_call(
        _proj_kernel,
        grid=(nb, nt),
        in_specs=[rows(D_MODEL), table, table] + [full(w) for w in weights],
        out_specs=out_specs,
        out_shape=out_shape,
        compiler_params=pltpu.CompilerParams(
            dimension_semantics=("parallel", "parallel"), vmem_limit_bytes=VMEM_LIMIT),
        name="proj",
    )(x3, cos_t, sin_t, *weights)


def _bias_lookup_kernel(rel_ref, bucket_ref, out_ref):
    bucket = bucket_ref[...]
    for head in range(H_A):
        acc = jnp.zeros(bucket.shape, F32)
        for b in range(N_BUCKETS):
            acc = jnp.where(bucket == b, rel_ref[b, head], acc)
        out_ref[head] = acc


def _bias_lookup(rel_bias, bucket):
    return pl.pallas_call(
        _bias_lookup_kernel,
        in_specs=[pl.BlockSpec(memory_space=pltpu.SMEM),
                  pl.BlockSpec(bucket.shape, lambda: (0, 0))],
        out_specs=pl.BlockSpec((H_A,) + bucket.shape, lambda: (0, 0, 0)),
        out_shape=jax.ShapeDtypeStruct((H_A,) + bucket.shape, F32),
        name="bias_lookup",
    )(rel_bias, bucket)


def _t5_bucket(n):
    n = np.asarray(n, np.int64)
    max_exact = N_BUCKETS // 2
    nf = np.maximum(n, 1).astype(np.float64)
    large = max_exact + (np.log(nf / max_exact) / math.log(MAX_DISTANCE / max_exact)
                         * (N_BUCKETS - max_exact)).astype(np.int64)
    return np.where(n < max_exact, n, np.minimum(large, N_BUCKETS - 1)).astype(np.int32)


def _diff_lambda(lam_ref):
    lp = lam_ref[...]
    s1 = jnp.sum(lp[0:1] * lp[1:2], axis=-1, keepdims=True)
    s2 = jnp.sum(lp[2:3] * lp[3:4], axis=-1, keepdims=True)
    return jnp.exp(s1) - jnp.exp(s2) + LAM_INIT


def _diff_head_finish(o1, o2, lam, subln_w):
    d = o1 - lam * o2
    return _rms(d, subln_w) * (1.0 - LAM_INIT)


def _mla_pair_finish(ob_even, ob_odd, wuv_ref, pair):
    return (_dot(ob_even.astype(BF16), wuv_ref[2 * pair])
            + _dot(ob_odd.astype(BF16), wuv_ref[2 * pair + 1]))


def _prompt_attn_kernel(qa_ref, qlat_ref, qpe_ref, ka_ref, va_ref, ckv_ref, kpe_ref, bias_ref,
                        lam_ref, subln_ref, wuv_ref, ya_ref, yb_ref, acc_ref, m_ref, l_ref):
    qi = pl.program_id(1)
    n_chunks = qi + 1
    lane = lax.broadcasted_iota(jnp.int32, (TQ, LANES), 1)
    row2 = lax.broadcasted_iota(jnp.int32, (2 * TQ, LANES), 0)
    lane2 = lax.broadcasted_iota(jnp.int32, (2 * TQ, LANES), 1)
    qrow2 = row2 & (TQ - 1)
    lam = _diff_lambda(lam_ref)
    subln_w = subln_ref[...]

    for head in range(H_A):
        hs = slice(head * LANES, (head + 1) * LANES)
        qh = qa_ref[:, hs]
        zero = jnp.zeros_like(qh)
        q2 = jnp.concatenate([jnp.where(lane < D_HEAD_A, qh, zero),
                              jnp.where(lane >= D_HEAD_A, qh, zero)], axis=0)

        def body(j, carry, hs=hs, q2=q2, head=head):
            m, l, acc = carry
            ks = pl.ds(pl.multiple_of(j * TQ, TQ), TQ)
            s = _nt_dot(q2, ka_ref[ks, hs])
            b = bias_ref[jnp.minimum(qi - j, 2), head]
            s = s * SCALE_A + jnp.concatenate([b, b], axis=0)
            s = jnp.where(qrow2 + (qi - j) * TQ >= lane2, s, NEG)
            m_new = jnp.maximum(m, jnp.max(s, axis=-1, keepdims=True))
            p = jnp.exp(s - m_new)
            corr = jnp.exp(m - m_new)
            l = l * corr + jnp.sum(p, axis=-1, keepdims=True)
            acc = acc * corr + _dot(p.astype(BF16), va_ref[ks, hs])
            return m_new, l, acc

        m0 = jnp.full((2 * TQ, 1), NEG, F32)
        l0 = jnp.zeros((2 * TQ, 1), F32)
        acc0 = jnp.zeros((2 * TQ, D_V_A), F32)
        _, l, acc = lax.fori_loop(0, n_chunks, body, (m0, l0, acc0))
        o = acc / l
        ya_ref[:, hs] = _diff_head_finish(o[0:TQ], o[TQ:2 * TQ], lam, subln_w).astype(ya_ref.dtype)

    rows = H_B * TQ
    q_lat = qlat_ref[...].reshape(rows, KV_LORA)
    group = lane >> 5
    q_pe = jnp.concatenate(
        [jnp.where(group == (head % 4), qpe_ref[:, (head // 4) * LANES:(head // 4 + 1) * LANES],
                   jnp.zeros((TQ, LANES), qpe_ref.dtype)) for head in range(H_B)], axis=0)
    rowm = lax.broadcasted_iota(jnp.int32, (rows, LANES), 0) & (TQ - 1)
    lanem = lax.broadcasted_iota(jnp.int32, (rows, LANES), 1)
    m_ref[...] = jnp.full(m_ref.shape, NEG, F32)
    l_ref[...] = jnp.zeros(l_ref.shape, F32)
    acc_ref[...] = jnp.zeros(acc_ref.shape, F32)

    def mla_body(j, carry):
        ks = pl.ds(pl.multiple_of(j * TQ, TQ), TQ)
        ckv = ckv_ref[ks, :]
        s = (_nt_dot(q_lat, ckv) + _nt_dot(q_pe, kpe_ref[ks, :])) * SCALE_B
        s = jnp.where(rowm + (qi - j) * TQ >= lanem, s, NEG)
        m = m_ref[...]
        m_new = jnp.maximum(m, jnp.max(s, axis=-1, keepdims=True))
        p = jnp.exp(s - m_new)
        corr = jnp.exp(m - m_new)
        l_ref[...] = l_ref[...] * corr + jnp.sum(p, axis=-1, keepdims=True)
        acc_ref[...] = acc_ref[...] * corr + _dot(p.astype(BF16), ckv)
        m_ref[...] = m_new
        return carry

    lax.fori_loop(0, n_chunks, mla_body, 0)
    ob = acc_ref[...] / l_ref[...]
    for pair in range(H_B // 2):
        e0 = 2 * pair * TQ
        yb_ref[:, pair * LANES:(pair + 1) * LANES] = _mla_pair_finish(
            ob[e0:e0 + TQ], ob[e0 + TQ:e0 + 2 * TQ], wuv_ref, pair).astype(yb_ref.dtype)


def _prompt_attention(qa, qlat, qpe, kab, vab, ckvb, kpeb, bias_tiles, lam_p, subln_w, wuv):
    nb = qa.shape[0]
    nq = T_PAD // TQ
    qrows = lambda width: pl.BlockSpec((None, TQ, width), lambda b, i: (b, i, 0))
    keys = lambda width: pl.BlockSpec((None, T_PAD, width), lambda b, i: (b, 0, 0))
    full = lambda a: pl.BlockSpec(a.shape, lambda b, i: (0,) * a.ndim)
    return pl.pallas_call(
        _prompt_attn_kernel,
        grid=(nb, nq),
        in_specs=[qrows(W_A),
                  pl.BlockSpec((H_B, None, TQ, KV_LORA), lambda b, i: (0, b, i, 0)),
                  qrows(Q_ROPE_COLS),
                  keys(W_A), keys(W_A), keys(KV_LORA), keys(LANES),
                  full(bias_tiles), full(lam_p), full(subln_w), full(wuv)],
        out_specs=(qrows(H_A * D_V_A), qrows(H_B * D_V_B)),
        out_shape=(jax.ShapeDtypeStruct((nb, T_PAD, H_A * D_V_A), BF16),
                   jax.ShapeDtypeStruct((nb, T_PAD, H_B * D_V_B), BF16)),
        scratch_shapes=[pltpu.VMEM((H_B * TQ, KV_LORA), F32),
                        pltpu.VMEM((H_B * TQ, 1), F32),
                        pltpu.VMEM((H_B * TQ, 1), F32)],
        compiler_params=pltpu.CompilerParams(
            dimension_semantics=("parallel", "parallel"), vmem_limit_bytes=VMEM_LIMIT),
        name="prompt_attn",
    )(qa, qlat, qpe, kab, vab, ckvb, kpeb, bias_tiles, lam_p, subln_w, wuv)


N_ROWS = 16


def _paged_attn_kernel(pt_ref, qd_ref, qm1_ref, qm2_ref, scale_ref, bias_ref, bself_ref,
                       knew_ref, vnew_ref, cnew_ref, rnew_ref, *refs):
    g = PAGES_PER_STEP
    k_refs, v_refs, c_refs, r_refs = refs[0:g], refs[g:2 * g], refs[2 * g:3 * g], refs[3 * g:4 * g]
    oa_ref, ob_ref, m_ref, l_ref, acca_ref, accb_ref = refs[4 * g:]
    step = pl.program_id(1)
    n_steps = pl.num_programs(1)

    @pl.when(step == 0)
    def _():
        m_ref[...] = jnp.full(m_ref.shape, NEG, F32)
        l_ref[...] = jnp.zeros(l_ref.shape, F32)
        acca_ref[...] = jnp.zeros(acca_ref.shape, F32)
        accb_ref[...] = jnp.zeros(accb_ref.shape, F32)

    qd = qd_ref[...]
    qm1 = qm1_ref[...]
    qm2 = qm2_ref[...]
    scale = scale_ref[...]
    parts = [_nt_dot(qd, k_refs[j][...]) + _nt_dot(qm1, c_refs[j][...]) + _nt_dot(qm2, r_refs[j][...])
             for j in range(g)]
    s = jnp.concatenate(parts, axis=1)
    width = g * PAGE_SIZE
    s = s * scale + bias_ref[:, pl.ds(pl.multiple_of(step * width, width), width)]
    m = m_ref[...]
    m_new = jnp.maximum(m, jnp.max(s, axis=-1, keepdims=True))
    p = jnp.exp(s - m_new)
    corr = jnp.exp(m - m_new)
    l_ref[...] = l_ref[...] * corr + jnp.sum(p, axis=-1, keepdims=True)
    half = N_ROWS // 2
    pa = _dot(p[0:half, 0:PAGE_SIZE], v_refs[0][...])
    pb = _dot(p[half:N_ROWS, 0:PAGE_SIZE], c_refs[0][...])
    for j in range(1, g):
        cols = slice(j * PAGE_SIZE, (j + 1) * PAGE_SIZE)
        pa = pa + _dot(p[0:half, cols], v_refs[j][...])
        pb = pb + _dot(p[half:N_ROWS, cols], c_refs[j][...])
    acca_ref[...] = acca_ref[...] * corr[0:half] + pa
    accb_ref[...] = accb_ref[...] * corr[half:N_ROWS] + pb
    m_ref[...] = m_new

    @pl.when(step == n_steps - 1)
    def _():
        s_new = (jnp.sum(qd * knew_ref[...], axis=-1, keepdims=True)
                 + jnp.sum(qm1 * cnew_ref[...], axis=-1, keepdims=True)
                 + jnp.sum(qm2 * rnew_ref[...], axis=-1, keepdims=True)) * scale + bself_ref[...]
        m1 = m_ref[...]
        m2 = jnp.maximum(m1, s_new)
        p_new = jnp.exp(s_new - m2)
        corr2 = jnp.exp(m1 - m2)
        l = l_ref[...] * corr2 + p_new
        acca = (acca_ref[...] * corr2[0:half] + p_new[0:half] * vnew_ref[...]) / l[0:half]
        accb = (accb_ref[...] * corr2[half:N_ROWS] + p_new[half:N_ROWS] * cnew_ref[...]) / l[half:N_ROWS]
        for r in range(half):
            head = r // 2
            oa_ref[r:r + 1, :] = acca[r:r + 1, head * D_V_A:(head + 1) * D_V_A]
        ob_ref[...] = accb


def _paged_attention(page_table, qd, qm1, qm2, scale, bias, bias_self, k_new, v_new, c_new, r_new,
                     cache_k, cache_v, cache_c, cache_r):
    g = PAGES_PER_STEP
    n_steps = N_PAGES // g
    per_seq = lambda a: pl.BlockSpec((None,) + a.shape[1:], lambda b, s, pt: (b,) + (0,) * (a.ndim - 1))
    full = lambda a: pl.BlockSpec(a.shape, lambda b, s, pt: (0,) * a.ndim)

    def page_spec(cache, j):
        return pl.BlockSpec((None,) + cache.shape[1:],
                            lambda b, s, pt, j=j: (pt[b * N_PAGES + s * g + j], 0, 0))

    page_specs = [page_spec(c, j) for c in (cache_k, cache_v, cache_c, cache_r) for j in range(g)]
    page_args = [c for c in (cache_k, cache_v, cache_c, cache_r) for _ in range(g)]
    grid_spec = pltpu.PrefetchScalarGridSpec(
        num_scalar_prefetch=1,
        grid=(DEC_BATCH, n_steps),
        in_specs=[per_seq(qd), per_seq(qm1), per_seq(qm2), full(scale), full(bias), full(bias_self),
                  per_seq(k_new), per_seq(v_new), per_seq(c_new), per_seq(r_new)] + page_specs,
        out_specs=(pl.BlockSpec((None, N_ROWS // 2, D_V_A), lambda b, s, pt: (b, 0, 0)),
                   pl.BlockSpec((None, N_ROWS // 2, KV_LORA), lambda b, s, pt: (b, 0, 0))),
        scratch_shapes=[pltpu.VMEM((N_ROWS, 1), F32), pltpu.VMEM((N_ROWS, 1), F32),
                        pltpu.VMEM((N_ROWS // 2, H_A * D_V_A), F32),
                        pltpu.VMEM((N_ROWS // 2, KV_LORA), F32)],
    )
    return pl.pallas_call(
        _paged_attn_kernel,
        grid_spec=grid_spec,
        out_shape=(jax.ShapeDtypeStruct((DEC_BATCH, N_ROWS // 2, D_V_A), F32),
                   jax.ShapeDtypeStruct((DEC_BATCH, N_ROWS // 2, KV_LORA), F32)),
        compiler_params=pltpu.CompilerParams(
            dimension_semantics=("parallel", "arbitrary"), vmem_limit_bytes=VMEM_LIMIT),
        name="paged_attn",
    )(page_table, qd, qm1, qm2, scale, bias, bias_self, k_new, v_new, c_new, r_new, *page_args)


def _finish_kernel(oa_ref, ob_ref, lam_ref, subln_ref, wuv_ref, ya_ref, yb_ref):
    lam = _diff_lambda(lam_ref)
    subln_w = subln_ref[...]
    for head in range(H_A):
        ya_ref[:, head * D_V_A:(head + 1) * D_V_A] = _diff_head_finish(
            oa_ref[2 * head], oa_ref[2 * head + 1], lam, subln_w).astype(ya_ref.dtype)
    for pair in range(H_B // 2):
        yb_ref[:, pair * LANES:(pair + 1) * LANES] = _mla_pair_finish(
            ob_ref[2 * pair], ob_ref[2 * pair + 1], wuv_ref, pair).astype(yb_ref.dtype)


def _finish(oa_t, ob_t, lam_p, subln_w, wuv):
    n = oa_t.shape[1]
    full = lambda a: pl.BlockSpec(a.shape, lambda: (0,) * a.ndim)
    return pl.pallas_call(
        _finish_kernel,
        in_specs=[full(oa_t), full(ob_t), full(lam_p), full(subln_w), full(wuv)],
        out_specs=(pl.BlockSpec((n, H_A * D_V_A), lambda: (0, 0)),
                   pl.BlockSpec((n, H_B * D_V_B), lambda: (0, 0))),
        out_shape=(jax.ShapeDtypeStruct((n, H_A * D_V_A), BF16),
                   jax.ShapeDtypeStruct((n, H_B * D_V_B), BF16)),
        name="sample_finish",
    )(oa_t, ob_t, lam_p, subln_w, wuv)


def _ffn_kernel(sample, *refs):
    if sample:
        (x_ref, ya_ref, yb_ref, s0_ref, s1_ref, nw_pre, wgate, woa, wob, wo, nw_post, nw_fpre, wup,
         convw, convb, wdown, nw_fpost, y_ref, aux_ref) = refs
    else:
        (x_ref, ya_ref, yb_ref, nw_pre, wgate, woa, wob, wo, nw_post, nw_fpre, wup,
         convw, convb, wdown, nw_fpost, y_ref, aux_ref, carry_ref) = refs
    x = x_ref[...]
    tm = x.shape[0]
    n = _rms(x, nw_pre[...]).astype(BF16)
    gate = jax.nn.sigmoid(_dot(n, wgate[...]))
    merged = (gate[:, 0:D_MODEL] * _dot(ya_ref[...], woa[...])
              + gate[:, D_MODEL:2 * D_MODEL] * _dot(yb_ref[...], wob[...]))
    mix = _dot(merged.astype(BF16), wo[...])
    h = x + _rms(mix, nw_post[...])
    n2 = _rms(h, nw_fpre[...]).astype(BF16)
    up = _dot(n2, wup[...])
    a = up[:, 0:D_FF]
    b = up[:, D_FF:2 * D_FF]
    cw = convw[...]
    if sample:
        prev2 = s0_ref[...]
        prev1 = s1_ref[...]
        aux_ref[...] = a
    else:
        ti = pl.program_id(1)

        @pl.when(ti == 0)
        def _():
            carry_ref[...] = jnp.zeros(carry_ref.shape, F32)

        row = lax.broadcasted_iota(jnp.int32, (tm, 1), 0)
        pos = ti * tm + row
        carry = carry_ref[...]
        prev1 = jnp.where(row == 0, carry[7:8], pltpu.roll(a, 1, 0))
        prev2 = jnp.where(row == 0, carry[6:7], jnp.where(row == 1, carry[7:8], pltpu.roll(a, 2, 0)))
        prev1 = jnp.where(pos >= 1, prev1, 0.0)
        prev2 = jnp.where(pos >= 2, prev2, 0.0)
        carry_ref[...] = a[tm - 8:tm]
        last = T_P - 1 - (T_PAD - tm)

        @pl.when(ti == pl.num_programs(1) - 1)
        def _():
            aux_ref[...] = a[last - 1:last + 1]

    conv = convb[...] + cw[0:1] * prev2
    conv = conv + cw[1:2] * prev1
    conv = conv + cw[2:3] * a
    hh = jax.nn.gelu(conv) * b
    f = _dot(hh.astype(BF16), wdown[...])
    y_ref[...] = h + _rms(f, nw_fpost[...])


def _ffn(x3, ya3, yb3, wts, tm, state=None):
    nb, t, _ = x3.shape
    nt = t // tm
    sample = state is not None
    rows = lambda width: pl.BlockSpec((None, tm, width), lambda b, i: (b, i, 0))
    full = lambda a: pl.BlockSpec(a.shape, lambda b, i: (0,) * a.ndim,
                                  pipeline_mode=pl.Buffered(1))
    weights = (wts['norm_mix_pre'], wts['w_gate'], wts['w_oa'], wts['w_ob'], wts['w_o'],
               wts['norm_mix_post'], wts['norm_ffn_pre'], wts['w_up'], wts['conv_w'], wts['conv_b'],
               wts['w_down'], wts['norm_ffn_post'])
    acts = [x3, ya3, yb3]
    act_specs = [rows(D_MODEL), rows(H_A * D_V_A), rows(H_B * D_V_B)]
    if sample:
        acts += [state[0], state[1]]
        act_specs += [rows(D_FF), rows(D_FF)]
        aux_shape = jax.ShapeDtypeStruct((nb, t, D_FF), F32)
        aux_spec = rows(D_FF)
        scratch = []
    else:
        aux_shape = jax.ShapeDtypeStruct((nb, 2, D_FF), F32)
        aux_spec = pl.BlockSpec((None, 2, D_FF), lambda b, i: (b, 0, 0))
        scratch = [pltpu.VMEM((8, D_FF), F32)]
    return pl.pallas_call(
        functools.partial(_ffn_kernel, sample),
        grid=(nb, nt),
        in_specs=act_specs + [full(w) for w in weights],
        out_specs=(rows(D_MODEL), aux_spec),
        out_shape=(jax.ShapeDtypeStruct((nb, t, D_MODEL), F32), aux_shape),
        scratch_shapes=scratch,
        compiler_params=pltpu.CompilerParams(
            dimension_semantics=("parallel", "arbitrary"), vmem_limit_bytes=VMEM_LIMIT),
        name="ffn_sample" if sample else "ffn_prompt",
    )(*acts, *weights)


def _rope_tables(pos):
    half = D_ROPE // 2
    freqs = ROPE_BASE ** (-jnp.arange(half, dtype=F32) / half)
    ang = pos.astype(F32)[:, None] * freqs
    cos, sin = jnp.cos(ang), jnp.sin(ang)
    return (jnp.tile(jnp.concatenate([cos, cos], axis=-1), (1, H_B)),
            jnp.tile(jnp.concatenate([-sin, sin], axis=-1), (1, H_B)))


def _prepare_weights(norm_mix_pre, norm_mix_post, norm_ffn_pre, norm_ffn_post, w_in, w_gate, q_norm_w,
                     kv_norm_w, w_uq, w_ukv, w_oa, w_ob, w_o, w_up, conv_w, conv_b, w_down):
    row = lambda v: v[0].reshape(1, -1)
    w_in0 = w_in[0]
    w_in_aug = jnp.concatenate([w_in0] + [w_in0[:, IN_KPE:]] * 3, axis=1).astype(BF16)
    w_uq3 = w_uq[0].reshape(Q_LORA, H_B, D_NOPE + D_ROPE)
    w_uq_perm = jnp.concatenate([w_uq3[:, :, :D_NOPE].reshape(Q_LORA, Q_NOPE_COLS),
                                 w_uq3[:, :, D_NOPE:].reshape(Q_LORA, Q_ROPE_COLS)], axis=1).astype(BF16)
    w_ukv0 = w_ukv[0]
    uk = jnp.transpose(w_ukv0[:, :, :D_NOPE], (1, 2, 0))
    uv = jnp.transpose(w_ukv0[:, :, D_NOPE:], (1, 0, 2))
    zk = jnp.zeros_like(uk)
    zv = jnp.zeros_like(uv)
    even = (jnp.arange(H_B) % 2 == 0)
    w_uk = jnp.where(even[:, None, None], jnp.concatenate([uk, zk], axis=1),
                     jnp.concatenate([zk, uk], axis=1)).astype(BF16)
    w_uv = jnp.where(even[:, None, None], jnp.concatenate([uv, zv], axis=2),
                     jnp.concatenate([zv, uv], axis=2)).astype(BF16)
    return dict(
        norm_mix_pre=row(norm_mix_pre), norm_mix_post=row(norm_mix_post),
        norm_ffn_pre=row(norm_ffn_pre), norm_ffn_post=row(norm_ffn_post),
        q_norm_w=row(q_norm_w), kv_norm_w=row(kv_norm_w),
        w_in=w_in_aug, w_uq=w_uq_perm, w_uk=w_uk, w_uv=w_uv,
        w_gate=w_gate[0].astype(BF16), w_oa=w_oa[0].astype(BF16), w_ob=w_ob[0].astype(BF16),
        w_o=w_o[0].astype(BF16), w_up=w_up[0].astype(BF16), w_down=w_down[0].astype(BF16),
        conv_w=conv_w[0], conv_b=row(conv_b))


def kernel(x_prompt, x_sample, cache_diff_k, cache_diff_v, cache_mla_latent, cache_mla_rope, state_conv, page_table, meta_tokens, rel_bias, norm_mix_pre, norm_mix_post, norm_ffn_pre, norm_ffn_post, w_in, w_gate, lambda_q1, lambda_k1, lambda_q2, lambda_k2, subln_w, q_norm_w, kv_norm_w, w_uq, w_ukv, w_oa, w_ob, w_o, w_up, conv_w, conv_b, w_down):
    assert x_prompt.shape == (BATCH, SEQ, D_MODEL) and x_sample.shape == (DEC_BATCH, 1, D_MODEL)
    assert page_table.shape == (DEC_BATCH, N_PAGES) and w_in.shape[0] == 1
    wts = _prepare_weights(norm_mix_pre, norm_mix_post, norm_ffn_pre, norm_ffn_post, w_in, w_gate,
                           q_norm_w, kv_norm_w, w_uq, w_ukv, w_oa, w_ob, w_o, w_up, conv_w, conv_b,
                           w_down)
    lam_p = jnp.concatenate([lambda_q1, lambda_k1, lambda_q2, lambda_k2], axis=0)
    subln = subln_w[0].reshape(1, D_V_A)

    i = np.arange(TQ)
    tile_dist = np.stack([np.maximum(d * TQ + i[:, None] - i[None, :], 0) for d in range(3)])
    bias_tiles = _bias_lookup(rel_bias, jnp.asarray(_t5_bucket(tile_dist).reshape(3 * TQ, TQ)))
    bias_tiles = bias_tiles.reshape(H_A, 3, TQ, TQ).transpose(1, 0, 2, 3)
    past_dist = PAST_LEN - np.arange(PAST_LEN + PAGE_SIZE)
    past_dist = np.where(past_dist >= 0, past_dist, 0)
    past_bias = _bias_lookup(rel_bias, jnp.asarray(_t5_bucket(past_dist).reshape(N_PAGES + 1, PAGE_SIZE)))
    past_bias = past_bias.reshape(H_A, PAST_LEN + PAGE_SIZE)
    bias_rows = jnp.concatenate([jnp.repeat(past_bias, 2, axis=0),
                                 jnp.zeros((N_ROWS // 2, PAST_LEN + PAGE_SIZE), F32)], axis=0)
    bias_past = bias_rows[:, :PAST_LEN]
    bias_self = bias_rows[:, PAST_LEN:PAST_LEN + 1]

    meta = jnp.broadcast_to(meta_tokens.astype(x_prompt.dtype)[None], (BATCH, N_META, D_MODEL))
    hp = jnp.concatenate([meta, x_prompt, jnp.zeros((BATCH, T_PAD - T_P, D_MODEL), x_prompt.dtype)],
                         axis=1)
    cos_p, sin_p = _rope_tables(jnp.arange(T_PAD, dtype=jnp.int32))
    (qa, qlat, qpe, ka, va, ckv, kpe, kab, vab, ckvb, kpeb) = _project(
        hp, cos_p, sin_p, wts, TM_DENSE, BF16)
    ya, yb = _prompt_attention(qa, qlat, qpe, kab, vab, ckvb, kpeb, bias_tiles, lam_p, subln,
                               wts['w_uv'])
    y_p, conv_p = _ffn(hp, ya, yb, wts, TM_DENSE)

    xs = x_sample.reshape(1, DEC_BATCH, D_MODEL)
    cos_s, sin_s = _rope_tables(jnp.full((DEC_BATCH,), PAST_LEN, jnp.int32))
    (qa_s, qlat_s, qpe_s, ka_s, va_s, ckv_s, kpe_s, _, _, _, _) = _project(
        xs, cos_s, sin_s, wts, DEC_BATCH, F32)
    col = np.arange(W_A)
    row_of_col = (col // (2 * D_HEAD_A)) * 2 + (col % (2 * D_HEAD_A)) // D_HEAD_A
    own = jnp.asarray(row_of_col[None, :] == np.arange(N_ROWS)[:, None])
    qd = jnp.where(own[None], qa_s[0][:, None, :], 0.0)
    zero8 = lambda a: jnp.concatenate([jnp.zeros_like(a), a], axis=1)
    qm1 = zero8(jnp.transpose(qlat_s[:, 0], (1, 0, 2)))
    qm2 = zero8(qpe_s[0].reshape(DEC_BATCH, H_B, D_ROPE))
    scale = jnp.asarray(np.where(np.arange(N_ROWS) < N_ROWS // 2, SCALE_A, SCALE_B)
                        .astype(np.float32).reshape(N_ROWS, 1))
    n_pool = cache_diff_k.shape[1]
    oa_s, ob_s = _paged_attention(
        page_table.reshape(-1), qd, qm1, qm2, scale, bias_past, bias_self,
        ka_s.reshape(DEC_BATCH, 1, W_A), va_s.reshape(DEC_BATCH, 1, W_A),
        ckv_s.reshape(DEC_BATCH, 1, KV_LORA), kpe_s.reshape(DEC_BATCH, 1, D_ROPE),
        cache_diff_k[0].reshape(n_pool, PAGE_SIZE, W_A), cache_diff_v[0].reshape(n_pool, PAGE_SIZE, W_A),
        cache_mla_latent[0], cache_mla_rope[0])
    ya_s, yb_s = _finish(jnp.transpose(oa_s, (1, 0, 2)), jnp.transpose(ob_s, (1, 0, 2)), lam_p, subln,
                         wts['w_uv'])
    state = (state_conv[0, :, 0].reshape(1, DEC_BATCH, D_FF), state_conv[0, :, 1].reshape(1, DEC_BATCH, D_FF))
    y_s, a_s = _ffn(xs, ya_s.reshape(1, DEC_BATCH, -1), yb_s.reshape(1, DEC_BATCH, -1), wts, DEC_BATCH,
                    state=state)

    shape5 = lambda a, n, t: a.reshape(1, n, t, H_A, 2 * D_HEAD_A)
    return (
        y_p[:, N_META:T_P],
        y_s.reshape(DEC_BATCH, 1, D_MODEL),
        shape5(ka[:, :T_P], BATCH, T_P),
        shape5(va[:, :T_P], BATCH, T_P),
        ckv[:, :T_P][None],
        kpe[:, :T_P][None],
        conv_p[None],
        shape5(ka_s, DEC_BATCH, 1),
        shape5(va_s, DEC_BATCH, 1),
        ckv_s.reshape(1, DEC_BATCH, 1, KV_LORA),
        kpe_s.reshape(1, DEC_BATCH, 1, D_ROPE),
        jnp.stack([state_conv[0, :, 1], a_s[0]], axis=1)[None],
    )
```

```python
import functools
import math

import numpy as np
import jax
import jax.numpy as jnp
from jax import lax
from jax.experimental import pallas as pl
from jax.experimental.pallas import tpu as pltpu

F32 = jnp.float32
BF16 = jnp.bfloat16

D_MODEL = 1024
BATCH = 8
SEQ = 2048
DEC_BATCH = 128
PAST_LEN = 16384
PAGE_SIZE = 128
N_PAGES = PAST_LEN // PAGE_SIZE
N_META = 16
H_A = 4
D_HEAD_A = 64
D_V_A = 128
H_B = 8
Q_LORA = 384
KV_LORA = 256
D_NOPE = 64
D_ROPE = 32
D_V_B = 64
ROPE_BASE = 10000.0
N_BUCKETS = 32
MAX_DISTANCE = 128
D_FF = 2816
EPS = 1e-6
NEG = -1e30
SCALE_A = D_HEAD_A ** -0.5
SCALE_B = (D_NOPE + D_ROPE) ** -0.5
LAM_INIT = 0.8 - 0.6 * math.exp(-0.3 * 0)

T_P = SEQ + N_META
LANES = 128
T_PAD = 18 * LANES
TM_DENSE = T_PAD // 8
TQ = LANES
PAGES_PER_STEP = 8
VMEM_LIMIT = 56 * 1024 * 1024

W_A = H_A * 2 * D_HEAD_A
IN_QL = 3 * W_A
IN_KVL = IN_QL + Q_LORA
IN_KPE = IN_KVL + KV_LORA
IN_COLS = IN_KPE + LANES
Q_NOPE_COLS = H_B * D_NOPE
Q_ROPE_COLS = H_B * D_ROPE


def _rms(x, w):
    return x * lax.rsqrt(jnp.mean(x * x, axis=-1, keepdims=True) + EPS) * w


def _rope_lanes(x, c, s):
    lane = lax.broadcasted_iota(jnp.int32, x.shape, 1)
    from_right = pltpu.roll(x, LANES - 16, 1)
    from_left = pltpu.roll(x, 16, 1)
    swapped = jnp.where((lane & 16) == 0, from_right, from_left)
    return x * c + swapped * s


def _nt_dot(a, b):
    return lax.dot_general(a, b, (((1,), (1,)), ((), ())), preferred_element_type=F32)


def _dot(a, b):
    return jnp.dot(a, b, preferred_element_type=F32)


def _proj_kernel(x_ref, cos_ref, sin_ref, nw_ref, win_ref, qnw_ref, kvnw_ref, wuq_ref, wuk_ref,
                 qa_ref, qlat_ref, qpe_ref, ka_ref, va_ref, ckv_ref, kpe_ref,
                 kab_ref, vab_ref, ckvb_ref, kpeb_ref):
    x = x_ref[...]
    n = _rms(x, nw_ref[...]).astype(BF16)
    h = _dot(n, win_ref[...])
    qa_ref[...] = h[:, 0:W_A].astype(qa_ref.dtype)
    ka = h[:, W_A:2 * W_A]
    va = h[:, 2 * W_A:3 * W_A]
    ka_ref[...] = ka
    va_ref[...] = va
    kab_ref[...] = ka.astype(BF16)
    vab_ref[...] = va.astype(BF16)
    c = cos_ref[...]
    s = sin_ref[...]

    qn = _rms(h[:, IN_QL:IN_KVL], qnw_ref[...]).astype(BF16)
    q = _dot(qn, wuq_ref[...])
    for half in range(2):
        lo = Q_NOPE_COLS + half * LANES
        qpe_ref[:, half * LANES:(half + 1) * LANES] = _rope_lanes(
            q[:, lo:lo + LANES], c[:, half * LANES:(half + 1) * LANES],
            s[:, half * LANES:(half + 1) * LANES]).astype(qpe_ref.dtype)
    for pair in range(H_B // 2):
        qp = q[:, pair * LANES:(pair + 1) * LANES].astype(BF16)
        for head in (2 * pair, 2 * pair + 1):
            qlat_ref[head] = _dot(qp, wuk_ref[head]).astype(qlat_ref.dtype)

    ckv = _rms(h[:, IN_KVL:IN_KPE], kvnw_ref[...])
    ckv_ref[...] = ckv
    ckvb_ref[...] = ckv.astype(BF16)
    kpe4 = _rope_lanes(h[:, IN_KPE:IN_COLS], c[:, 0:LANES], s[:, 0:LANES])
    kpe_ref[...] = kpe4[:, 0:D_ROPE]
    kpeb_ref[...] = kpe4.astype(BF16)


def _project(x3, cos_t, sin_t, wts, tm, q_dtype):
    nb, t, _ = x3.shape
    nt = t // tm
    rows = lambda width: pl.BlockSpec((None, tm, width), lambda b, i: (b, i, 0))
    table = pl.BlockSpec((tm, Q_ROPE_COLS), lambda b, i: (i, 0))
    full = lambda a: pl.BlockSpec(a.shape, lambda b, i: (0,) * a.ndim)
    weights = (wts['norm_mix_pre'], wts['w_in'], wts['q_norm_w'], wts['kv_norm_w'], wts['w_uq'],
               wts['w_uk'])
    out_shape = (
        jax.ShapeDtypeStruct((nb, t, W_A), q_dtype),
        jax.ShapeDtypeStruct((H_B, nb, t, KV_LORA), q_dtype),
        jax.ShapeDtypeStruct((nb, t, Q_ROPE_COLS), q_dtype),
        jax.ShapeDtypeStruct((nb, t, W_A), F32),
        jax.ShapeDtypeStruct((nb, t, W_A), F32),
        jax.ShapeDtypeStruct((nb, t, KV_LORA), F32),
        jax.ShapeDtypeStruct((nb, t, D_ROPE), F32),
        jax.ShapeDtypeStruct((nb, t, W_A), BF16),
        jax.ShapeDtypeStruct((nb, t, W_A), BF16),
        jax.ShapeDtypeStruct((nb, t, KV_LORA), BF16),
        jax.ShapeDtypeStruct((nb, t, LANES), BF16),
    )
    out_specs = (
        rows(W_A),
        pl.BlockSpec((H_B, None, tm, KV_LORA), lambda b, i: (0, b, i, 0)),
        rows(Q_ROPE_COLS), rows(W_A), rows(W_A), rows(KV_LORA), rows(D_ROPE),
        rows(W_A), rows(W_A), rows(KV_LORA), rows(LANES),
    )
    return pl.pallas_call(
        _proj_kernel,
        grid=(nb, nt),
        in_specs=[rows(D_MODEL), table, table] + [full(w) for w in weights],
        out_specs=out_specs,
        out_shape=out_shape,
        compiler_params=pltpu.CompilerParams(
            dimension_semantics=("parallel", "parallel"), vmem_limit_bytes=VMEM_LIMIT),
        name="proj",
    )(x3, cos_t, sin_t, *weights)


def _bias_lookup_kernel(rel_ref, bucket_ref, out_ref):
    bucket = bucket_ref[...]
    for head in range(H_A):
        acc = jnp.zeros(bucket.shape, F32)
        for b in range(N_BUCKETS):
            acc = jnp.where(bucket == b, rel_ref[b, head], acc)
        out_ref[head] = acc


def _bias_lookup(rel_bias, bucket):
    return pl.pallas_call(
        _bias_lookup_kernel,
        in_specs=[pl.BlockSpec(memory_space=pltpu.SMEM),
                  pl.BlockSpec(bucket.shape, lambda: (0, 0))],
        out_specs=pl.BlockSpec((H_A,) + bucket.shape, lambda: (0, 0, 0)),
        out_shape=jax.ShapeDtypeStruct((H_A,) + bucket.shape, F32),
        name="bias_lookup",
    )(rel_bias, bucket)


def _t5_bucket(n):
    n = np.asarray(n, np.int64)
    max_exact = N_BUCKETS // 2
    nf = np.maximum(n, 1).astype(np.float64)
    large = max_exact + (np.log(nf / max_exact) / math.log(MAX_DISTANCE / max_exact)
                         * (N_BUCKETS - max_exact)).astype(np.int64)
    return np.where(n < max_exact, n, np.minimum(large, N_BUCKETS - 1)).astype(np.int32)


def _diff_lambda(lam_ref):
    lp = lam_ref[...]
    s1 = jnp.sum(lp[0:1] * lp[1:2], axis=-1, keepdims=True)
    s2 = jnp.sum(lp[2:3] * lp[3:4], axis=-1, keepdims=True)
    return jnp.exp(s1) - jnp.exp(s2) + LAM_INIT


def _diff_head_finish(o1, o2, lam, subln_w):
    d = o1 - lam * o2
    return _rms(d, subln_w) * (1.0 - LAM_INIT)


def _mla_pair_finish(ob_even, ob_odd, wuv_ref, pair):
    return (_dot(ob_even.astype(BF16), wuv_ref[2 * pair])
            + _dot(ob_odd.astype(BF16), wuv_ref[2 * pair + 1]))


TK = 2 * LANES
N_Q_TILES = T_PAD // TQ
N_REAL_Q_TILES = -(-T_P // TQ)


def _prompt_attn_kernel(qa_ref, qlat_ref, qpe_ref, ka_ref, va_ref, ckv_ref, kpe_ref, bias_ref,
                        lam_ref, subln_ref, wuv_ref, ya_ref, yb_ref,
                        sd_ref, sm_ref, md_ref, mm_ref, accd_ref, ld_ref, accm_ref, lm_ref):
    qi = pl.program_id(1)

    @pl.when(qi >= N_REAL_Q_TILES)
    def _():
        ya_ref[...] = jnp.zeros(ya_ref.shape, ya_ref.dtype)
        yb_ref[...] = jnp.zeros(yb_ref.shape, yb_ref.dtype)

    @pl.when(qi < N_REAL_Q_TILES)
    def _():
        n_it = (qi + 2) // 2
        rows_m = H_B * TQ
        lane = lax.broadcasted_iota(jnp.int32, (TQ, LANES), 1)
        heads = [slice(h * LANES, (h + 1) * LANES) for h in range(H_A)]
        q2 = []
        for hs in heads:
            qh = qa_ref[:, hs] * SCALE_A
            zero = jnp.zeros_like(qh)
            q2.append(jnp.concatenate([jnp.where(lane < D_HEAD_A, qh, zero),
                                       jnp.where(lane >= D_HEAD_A, qh, zero)], axis=0))
        q_lat = qlat_ref[...].reshape(rows_m, KV_LORA)
        group = lane >> 5
        q_pe = jnp.concatenate(
            [jnp.where(group == (head % 4), qpe_ref[:, (head // 4) * LANES:(head // 4 + 1) * LANES],
                       jnp.zeros((TQ, LANES), qpe_ref.dtype)) for head in range(H_B)], axis=0)

        md_ref[...] = jnp.full(md_ref.shape, NEG, F32)
        mm_ref[...] = jnp.full(mm_ref.shape, NEG, F32)

        def scores(j, masked):
            ks = pl.ds(pl.multiple_of(j * TK, TK), TK)
            off = qi * TQ - j * TK
            d0 = jnp.clip(qi - 2 * j, 0, 2)
            d1 = jnp.clip(qi - 2 * j - 1, 0, 2)
            if masked:
                row_d = lax.broadcasted_iota(jnp.int32, (2 * TQ, TK), 0) & (TQ - 1)
                lane_d = lax.broadcasted_iota(jnp.int32, (2 * TQ, TK), 1)
                keep_d = row_d + off >= lane_d
                row_m = lax.broadcasted_iota(jnp.int32, (rows_m, TK), 0) & (TQ - 1)
                lane_m = lax.broadcasted_iota(jnp.int32, (rows_m, TK), 1)
                keep_m = row_m + off >= lane_m
            for h, hs in enumerate(heads):
                b = jnp.concatenate([bias_ref[d0, h], bias_ref[d1, h]], axis=1)
                s = _nt_dot(q2[h], ka_ref[ks, hs]) + jnp.concatenate([b, b], axis=0)
                if masked:
                    s = jnp.where(keep_d, s, NEG)
                sd_ref[h, :, ks] = s
                md_ref[h] = jnp.maximum(md_ref[h], jnp.maximum(s[:, 0:LANES], s[:, LANES:TK]))
            s = (_nt_dot(q_lat, ckv_ref[ks, :]) + _nt_dot(q_pe, kpe_ref[ks, :])) * SCALE_B
            if masked:
                s = jnp.where(keep_m, s, NEG)
            sm_ref[:, ks] = s
            mm_ref[...] = jnp.maximum(mm_ref[...], jnp.maximum(s[:, 0:LANES], s[:, LANES:TK]))

        def scores_body(j, carry):
            scores(j, False)
            return carry

        lax.fori_loop(0, n_it - 1, scores_body, 0)
        scores(n_it - 1, True)

        m_d = [jnp.broadcast_to(jnp.max(md_ref[h], axis=-1, keepdims=True), (2 * TQ, LANES))
               for h in range(H_A)]
        m_m = jnp.broadcast_to(jnp.max(mm_ref[...], axis=-1, keepdims=True), (rows_m, LANES))
        accd_ref[...] = jnp.zeros(accd_ref.shape, F32)
        ld_ref[...] = jnp.zeros(ld_ref.shape, F32)
        accm_ref[...] = jnp.zeros(accm_ref.shape, F32)
        lm_ref[...] = jnp.zeros(lm_ref.shape, F32)

        def values_body(j, carry):
            ks = pl.ds(pl.multiple_of(j * TK, TK), TK)
            for h, hs in enumerate(heads):
                p = jnp.exp(sd_ref[h, :, ks] - jnp.concatenate([m_d[h], m_d[h]], axis=1))
                ld_ref[h] = ld_ref[h] + (p[:, 0:LANES] + p[:, LANES:TK])
                accd_ref[h] = accd_ref[h] + _dot(p.astype(BF16), va_ref[ks, hs])
            p = jnp.exp(sm_ref[:, ks] - jnp.concatenate([m_m, m_m], axis=1))
            lm_ref[...] = lm_ref[...] + (p[:, 0:LANES] + p[:, LANES:TK])
            accm_ref[...] = accm_ref[...] + _dot(p.astype(BF16), ckv_ref[ks, :])
            return carry

        lax.fori_loop(0, n_it, values_body, 0)

        lam = _diff_lambda(lam_ref)
        subln_w = subln_ref[...]
        for h, hs in enumerate(heads):
            o = accd_ref[h] / jnp.sum(ld_ref[h], axis=-1, keepdims=True)
            ya_ref[:, hs] = _diff_head_finish(o[0:TQ], o[TQ:2 * TQ], lam, subln_w).astype(ya_ref.dtype)
        ob = accm_ref[...] / jnp.sum(lm_ref[...], axis=-1, keepdims=True)
        for pair in range(H_B // 2):
            e0 = 2 * pair * TQ
            yb_ref[:, pair * LANES:(pair + 1) * LANES] = _mla_pair_finish(
                ob[e0:e0 + TQ], ob[e0 + TQ:e0 + 2 * TQ], wuv_ref, pair).astype(yb_ref.dtype)


def _prompt_attention(qa, qlat, qpe, kab, vab, ckvb, kpeb, bias_tiles, lam_p, subln_w, wuv):
    nb = qa.shape[0]
    qrows = lambda width: pl.BlockSpec((None, TQ, width), lambda b, i: (b, i, 0))
    keys = lambda width: pl.BlockSpec((None, T_PAD, width), lambda b, i: (b, 0, 0))
    full = lambda a: pl.BlockSpec(a.shape, lambda b, i: (0,) * a.ndim)
    return pl.pallas_call(
        _prompt_attn_kernel,
        grid=(nb, N_Q_TILES),
        in_specs=[qrows(W_A),
                  pl.BlockSpec((H_B, None, TQ, KV_LORA), lambda b, i: (0, b, i, 0)),
                  qrows(Q_ROPE_COLS),
                  keys(W_A), keys(W_A), keys(KV_LORA), keys(LANES),
                  full(bias_tiles), full(lam_p), full(subln_w), full(wuv)],
        out_specs=(qrows(H_A * D_V_A), qrows(H_B * D_V_B)),
        out_shape=(jax.ShapeDtypeStruct((nb, T_PAD, H_A * D_V_A), BF16),
                   jax.ShapeDtypeStruct((nb, T_PAD, H_B * D_V_B), BF16)),
        scratch_shapes=[pltpu.VMEM((H_A, 2 * TQ, T_PAD), F32),
                        pltpu.VMEM((H_B * TQ, T_PAD), F32),
                        pltpu.VMEM((H_A, 2 * TQ, LANES), F32),
                        pltpu.VMEM((H_B * TQ, LANES), F32),
                        pltpu.VMEM((H_A, 2 * TQ, D_V_A), F32),
                        pltpu.VMEM((H_A, 2 * TQ, LANES), F32),
                        pltpu.VMEM((H_B * TQ, KV_LORA), F32),
                        pltpu.VMEM((H_B * TQ, LANES), F32)],
        compiler_params=pltpu.CompilerParams(
            dimension_semantics=("parallel", "parallel"), vmem_limit_bytes=VMEM_LIMIT),
        name="prompt_attn",
    )(qa, qlat, qpe, kab, vab, ckvb, kpeb, bias_tiles, lam_p, subln_w, wuv)


ROWS_PER_PAGE = PAGE_SIZE * H_A


def _online_update(s, m_ref, l_ref):
    m = m_ref[...]
    m_new = jnp.maximum(m, jnp.max(s, axis=-1, keepdims=True))
    p = jnp.exp(s - m_new)
    corr = jnp.exp(m - m_new)
    l_ref[...] = l_ref[...] * corr + jnp.sum(p, axis=-1, keepdims=True)
    m_ref[...] = m_new
    return p, corr


def _paged_attn_kernel(pt_ref, qd_ref, ql_ref, qr_ref, bias_ref, bself_ref,
                       knew_ref, vnew_ref, cnew_ref, rnew_ref, *refs):
    g = PAGES_PER_STEP
    k_refs, v_refs, c_refs, r_refs = refs[0:g], refs[g:2 * g], refs[2 * g:3 * g], refs[3 * g:4 * g]
    oa_ref, ob_ref, md_ref, ld_ref, mm_ref, lm_ref, acca_ref, accb_ref = refs[4 * g:]
    step = pl.program_id(1)

    @pl.when(step == 0)
    def _():
        md_ref[...] = jnp.full(md_ref.shape, NEG, F32)
        mm_ref[...] = jnp.full(mm_ref.shape, NEG, F32)
        ld_ref[...] = jnp.zeros(ld_ref.shape, F32)
        lm_ref[...] = jnp.zeros(lm_ref.shape, F32)
        acca_ref[...] = jnp.zeros(acca_ref.shape, F32)
        accb_ref[...] = jnp.zeros(accb_ref.shape, F32)

    qd = qd_ref[...]
    ql = ql_ref[...]
    qr = qr_ref[...]
    wd = g * ROWS_PER_PAGE
    sd = jnp.concatenate([_nt_dot(qd, k_refs[j][...]) for j in range(g)], axis=1)
    sd = sd * SCALE_A + bias_ref[:, pl.ds(pl.multiple_of(step * wd, wd), wd)]
    sm = jnp.concatenate([_nt_dot(ql, c_refs[j][...]) + _dot(qr, r_refs[j][...])
                          for j in range(g)], axis=1) * SCALE_B
    pd, corr_d = _online_update(sd, md_ref, ld_ref)
    pm, corr_m = _online_update(sm, mm_ref, lm_ref)
    pa = _dot(pd[:, 0:ROWS_PER_PAGE], v_refs[0][...])
    pb = _dot(pm[:, 0:PAGE_SIZE], c_refs[0][...])
    for j in range(1, g):
        pa = pa + _dot(pd[:, j * ROWS_PER_PAGE:(j + 1) * ROWS_PER_PAGE], v_refs[j][...])
        pb = pb + _dot(pm[:, j * PAGE_SIZE:(j + 1) * PAGE_SIZE], c_refs[j][...])
    acca_ref[...] = acca_ref[...] * corr_d + pa
    accb_ref[...] = accb_ref[...] * corr_m + pb

    @pl.when(step == pl.num_programs(1) - 1)
    def _():
        s_d = jnp.sum(qd * knew_ref[...], axis=-1, keepdims=True) * SCALE_A + bself_ref[...]
        s_m = (jnp.sum(ql * cnew_ref[...], axis=-1, keepdims=True)
               + jnp.sum(qr * rnew_ref[...], axis=-1, keepdims=True)) * SCALE_B
        p_d, c_d = _online_update(s_d, md_ref, ld_ref)
        p_m, c_m = _online_update(s_m, mm_ref, lm_ref)
        oa_ref[...] = (acca_ref[...] * c_d + p_d * vnew_ref[...]) / ld_ref[...]
        ob_ref[...] = (accb_ref[...] * c_m + p_m * cnew_ref[...]) / lm_ref[...]


def _paged_attention(page_table, qd, ql, qr, bias, bias_self, k_new, v_new, c_new, r_new,
                     cache_k, cache_v, cache_c, cache_r):
    g = PAGES_PER_STEP
    n_steps = N_PAGES // g
    half = H_A * 2
    per_seq = lambda a: pl.BlockSpec((None,) + a.shape[1:], lambda b, s, pt: (b,) + (0,) * (a.ndim - 1))
    full = lambda a: pl.BlockSpec(a.shape, lambda b, s, pt: (0,) * a.ndim)
    page_of = lambda pt, b, s, j: pt[b * N_PAGES + s * g + j]
    rows_spec = lambda j: pl.BlockSpec((ROWS_PER_PAGE, 2 * D_HEAD_A),
                                       lambda b, s, pt: (page_of(pt, b, s, j), 0))
    c_spec = lambda j: pl.BlockSpec((None, PAGE_SIZE, KV_LORA),
                                    lambda b, s, pt: (page_of(pt, b, s, j), 0, 0))
    r_spec = lambda j: pl.BlockSpec((None, D_ROPE, PAGE_SIZE),
                                    lambda b, s, pt: (page_of(pt, b, s, j), 0, 0))
    page_specs = ([rows_spec(j) for j in range(g)] + [rows_spec(j) for j in range(g)]
                  + [c_spec(j) for j in range(g)] + [r_spec(j) for j in range(g)])
    page_args = [cache_k] * g + [cache_v] * g + [cache_c] * g + [cache_r] * g
    grid_spec = pltpu.PrefetchScalarGridSpec(
        num_scalar_prefetch=1,
        grid=(DEC_BATCH, n_steps),
        in_specs=[per_seq(qd), per_seq(ql), per_seq(qr), full(bias), full(bias_self),
                  per_seq(k_new), per_seq(v_new), per_seq(c_new), per_seq(r_new)] + page_specs,
        out_specs=(pl.BlockSpec((None, half, D_V_A), lambda b, s, pt: (b, 0, 0)),
                   pl.BlockSpec((None, H_B, KV_LORA), lambda b, s, pt: (b, 0, 0))),
        scratch_shapes=[pltpu.VMEM((half, 1), F32), pltpu.VMEM((half, 1), F32),
                        pltpu.VMEM((H_B, 1), F32), pltpu.VMEM((H_B, 1), F32),
                        pltpu.VMEM((half, D_V_A), F32), pltpu.VMEM((H_B, KV_LORA), F32)],
    )
    return pl.pallas_call(
        _paged_attn_kernel,
        grid_spec=grid_spec,
        out_shape=(jax.ShapeDtypeStruct((DEC_BATCH, half, D_V_A), F32),
                   jax.ShapeDtypeStruct((DEC_BATCH, H_B, KV_LORA), F32)),
        compiler_params=pltpu.CompilerParams(
            dimension_semantics=("parallel", "arbitrary"), vmem_limit_bytes=VMEM_LIMIT),
        name="paged_attn",
    )(page_table, qd, ql, qr, bias, bias_self, k_new, v_new, c_new, r_new, *page_args)


def _finish_kernel(oa_ref, ob_ref, lam_ref, subln_ref, wuv_ref, ya_ref, yb_ref):
    lam = _diff_lambda(lam_ref)
    subln_w = subln_ref[...]
    for head in range(H_A):
        ya_ref[:, head * D_V_A:(head + 1) * D_V_A] = _diff_head_finish(
            oa_ref[2 * head], oa_ref[2 * head + 1], lam, subln_w).astype(ya_ref.dtype)
    for pair in range(H_B // 2):
        yb_ref[:, pair * LANES:(pair + 1) * LANES] = _mla_pair_finish(
            ob_ref[2 * pair], ob_ref[2 * pair + 1], wuv_ref, pair).astype(yb_ref.dtype)


def _finish(oa_t, ob_t, lam_p, subln_w, wuv):
    n = oa_t.shape[1]
    full = lambda a: pl.BlockSpec(a.shape, lambda: (0,) * a.ndim)
    return pl.pallas_call(
        _finish_kernel,
        in_specs=[full(oa_t), full(ob_t), full(lam_p), full(subln_w), full(wuv)],
        out_specs=(pl.BlockSpec((n, H_A * D_V_A), lambda: (0, 0)),
                   pl.BlockSpec((n, H_B * D_V_B), lambda: (0, 0))),
        out_shape=(jax.ShapeDtypeStruct((n, H_A * D_V_A), BF16),
                   jax.ShapeDtypeStruct((n, H_B * D_V_B), BF16)),
        name="sample_finish",
    )(oa_t, ob_t, lam_p, subln_w, wuv)


def _ffn_kernel(sample, *refs):
    if sample:
        (x_ref, ya_ref, yb_ref, s0_ref, s1_ref, nw_pre, wgate, woa, wob, wo, nw_post, nw_fpre, wup,
         convw, convb, wdown, nw_fpost, y_ref, aux_ref) = refs
    else:
        (x_ref, ya_ref, yb_ref, nw_pre, wgate, woa, wob, wo, nw_post, nw_fpre, wup,
         convw, convb, wdown, nw_fpost, y_ref, aux_ref, carry_ref) = refs
    x = x_ref[...]
    tm = x.shape[0]
    n = _rms(x, nw_pre[...]).astype(BF16)
    gate = jax.nn.sigmoid(_dot(n, wgate[...]))
    merged = (gate[:, 0:D_MODEL] * _dot(ya_ref[...], woa[...])
              + gate[:, D_MODEL:2 * D_MODEL] * _dot(yb_ref[...], wob[...]))
    mix = _dot(merged.astype(BF16), wo[...])
    h = x + _rms(mix, nw_post[...])
    n2 = _rms(h, nw_fpre[...]).astype(BF16)
    up = _dot(n2, wup[...])
    a = up[:, 0:D_FF]
    b = up[:, D_FF:2 * D_FF]
    cw = convw[...]
    if sample:
        prev2 = s0_ref[...]
        prev1 = s1_ref[...]
        aux_ref[...] = a
    else:
        ti = pl.program_id(1)

        @pl.when(ti == 0)
        def _():
            carry_ref[...] = jnp.zeros(carry_ref.shape, F32)

        row = lax.broadcasted_iota(jnp.int32, (tm, 1), 0)
        pos = ti * tm + row
        carry = carry_ref[...]
        prev1 = jnp.where(row == 0, carry[7:8], pltpu.roll(a, 1, 0))
        prev2 = jnp.where(row == 0, carry[6:7], jnp.where(row == 1, carry[7:8], pltpu.roll(a, 2, 0)))
        prev1 = jnp.where(pos >= 1, prev1, 0.0)
        prev2 = jnp.where(pos >= 2, prev2, 0.0)
        carry_ref[...] = a[tm - 8:tm]
        last = T_P - 1 - (T_PAD - tm)

        @pl.when(ti == pl.num_programs(1) - 1)
        def _():
            aux_ref[...] = a[last - 1:last + 1]

    conv = convb[...] + cw[0:1] * prev2
    conv = conv + cw[1:2] * prev1
    conv = conv + cw[2:3] * a
    hh = jax.nn.gelu(conv) * b
    f = _dot(hh.astype(BF16), wdown[...])
    y_ref[...] = h + _rms(f, nw_fpost[...])


def _ffn(x3, ya3, yb3, wts, tm, state=None):
    nb, t, _ = x3.shape
    nt = t // tm
    sample = state is not None
    rows = lambda width: pl.BlockSpec((None, tm, width), lambda b, i: (b, i, 0))
    full = lambda a: pl.BlockSpec(a.shape, lambda b, i: (0,) * a.ndim,
                                  pipeline_mode=pl.Buffered(1))
    weights = (wts['norm_mix_pre'], wts['w_gate'], wts['w_oa'], wts['w_ob'], wts['w_o'],
               wts['norm_mix_post'], wts['norm_ffn_pre'], wts['w_up'], wts['conv_w'], wts['conv_b'],
               wts['w_down'], wts['norm_ffn_post'])
    acts = [x3, ya3, yb3]
    act_specs = [rows(D_MODEL), rows(H_A * D_V_A), rows(H_B * D_V_B)]
    if sample:
        acts += [state[0], state[1]]
        act_specs += [rows(D_FF), rows(D_FF)]
        aux_shape = jax.ShapeDtypeStruct((nb, t, D_FF), F32)
        aux_spec = rows(D_FF)
        scratch = []
    else:
        aux_shape = jax.ShapeDtypeStruct((nb, 2, D_FF), F32)
        aux_spec = pl.BlockSpec((None, 2, D_FF), lambda b, i: (b, 0, 0))
        scratch = [pltpu.VMEM((8, D_FF), F32)]
    return pl.pallas_call(
        functools.partial(_ffn_kernel, sample),
        grid=(nb, nt),
        in_specs=act_specs + [full(w) for w in weights],
        out_specs=(rows(D_MODEL), aux_spec),
        out_shape=(jax.ShapeDtypeStruct((nb, t, D_MODEL), F32), aux_shape),
        scratch_shapes=scratch,
        compiler_params=pltpu.CompilerParams(
            dimension_semantics=("parallel", "arbitrary"), vmem_limit_bytes=VMEM_LIMIT),
        name="ffn_sample" if sample else "ffn_prompt",
    )(*acts, *weights)


def _rope_tables(pos):
    half = D_ROPE // 2
    freqs = ROPE_BASE ** (-jnp.arange(half, dtype=F32) / half)
    ang = pos.astype(F32)[:, None] * freqs
    cos, sin = jnp.cos(ang), jnp.sin(ang)
    return (jnp.tile(jnp.concatenate([cos, cos], axis=-1), (1, H_B)),
            jnp.tile(jnp.concatenate([-sin, sin], axis=-1), (1, H_B)))


def _prepare_weights(norm_mix_pre, norm_mix_post, norm_ffn_pre, norm_ffn_post, w_in, w_gate, q_norm_w,
                     kv_norm_w, w_uq, w_ukv, w_oa, w_ob, w_o, w_up, conv_w, conv_b, w_down):
    row = lambda v: v[0].reshape(1, -1)
    w_in0 = w_in[0]
    w_in_aug = jnp.concatenate([w_in0] + [w_in0[:, IN_KPE:]] * 3, axis=1).astype(BF16)
    w_uq3 = w_uq[0].reshape(Q_LORA, H_B, D_NOPE + D_ROPE)
    w_uq_perm = jnp.concatenate([w_uq3[:, :, :D_NOPE].reshape(Q_LORA, Q_NOPE_COLS),
                                 w_uq3[:, :, D_NOPE:].reshape(Q_LORA, Q_ROPE_COLS)], axis=1).astype(BF16)
    w_ukv0 = w_ukv[0]
    uk = jnp.transpose(w_ukv0[:, :, :D_NOPE], (1, 2, 0))
    uv = jnp.transpose(w_ukv0[:, :, D_NOPE:], (1, 0, 2))
    zk = jnp.zeros_like(uk)
    zv = jnp.zeros_like(uv)
    even = (jnp.arange(H_B) % 2 == 0)
    w_uk = jnp.where(even[:, None, None], jnp.concatenate([uk, zk], axis=1),
                     jnp.concatenate([zk, uk], axis=1)).astype(BF16)
    w_uv = jnp.where(even[:, None, None], jnp.concatenate([uv, zv], axis=2),
                     jnp.concatenate([zv, uv], axis=2)).astype(BF16)
    return dict(
        norm_mix_pre=row(norm_mix_pre), norm_mix_post=row(norm_mix_post),
        norm_ffn_pre=row(norm_ffn_pre), norm_ffn_post=row(norm_ffn_post),
        q_norm_w=row(q_norm_w), kv_norm_w=row(kv_norm_w),
        w_in=w_in_aug, w_uq=w_uq_perm, w_uk=w_uk, w_uv=w_uv,
        w_gate=w_gate[0].astype(BF16), w_oa=w_oa[0].astype(BF16), w_ob=w_ob[0].astype(BF16),
        w_o=w_o[0].astype(BF16), w_up=w_up[0].astype(BF16), w_down=w_down[0].astype(BF16),
        conv_w=conv_w[0], conv_b=row(conv_b))


def kernel(x_prompt, x_sample, cache_diff_k, cache_diff_v, cache_mla_latent, cache_mla_rope, state_conv, page_table, meta_tokens, rel_bias, norm_mix_pre, norm_mix_post, norm_ffn_pre, norm_ffn_post, w_in, w_gate, lambda_q1, lambda_k1, lambda_q2, lambda_k2, subln_w, q_norm_w, kv_norm_w, w_uq, w_ukv, w_oa, w_ob, w_o, w_up, conv_w, conv_b, w_down):
    assert x_prompt.shape == (BATCH, SEQ, D_MODEL) and x_sample.shape == (DEC_BATCH, 1, D_MODEL)
    assert page_table.shape == (DEC_BATCH, N_PAGES) and w_in.shape[0] == 1
    wts = _prepare_weights(norm_mix_pre, norm_mix_post, norm_ffn_pre, norm_ffn_post, w_in, w_gate,
                           q_norm_w, kv_norm_w, w_uq, w_ukv, w_oa, w_ob, w_o, w_up, conv_w, conv_b,
                           w_down)
    lam_p = jnp.concatenate([lambda_q1, lambda_k1, lambda_q2, lambda_k2], axis=0)
    subln = subln_w[0].reshape(1, D_V_A)

    i = np.arange(TQ)
    tile_dist = np.stack([np.maximum(d * TQ + i[:, None] - i[None, :], 0) for d in range(3)])
    bias_tiles = _bias_lookup(rel_bias, jnp.asarray(_t5_bucket(tile_dist).reshape(3 * TQ, TQ)))
    bias_tiles = bias_tiles.reshape(H_A, 3, TQ, TQ).transpose(1, 0, 2, 3)
    past_dist = PAST_LEN - np.arange(PAST_LEN + PAGE_SIZE)
    past_dist = np.where(past_dist >= 0, past_dist, 0)
    past_bias = _bias_lookup(rel_bias, jnp.asarray(_t5_bucket(past_dist).reshape(N_PAGES + 1, PAGE_SIZE)))
    past_bias = past_bias.reshape(H_A, PAST_LEN + PAGE_SIZE)
    same_head = jnp.asarray(np.eye(H_A, dtype=bool))
    bias_past = jnp.where(same_head[:, None, :], past_bias[:, :PAST_LEN, None], NEG)
    bias_past = jnp.repeat(bias_past.reshape(H_A, PAST_LEN * H_A), 2, axis=0)
    bias_self = jnp.repeat(past_bias[:, PAST_LEN:PAST_LEN + 1], 2, axis=0)

    meta = jnp.broadcast_to(meta_tokens.astype(x_prompt.dtype)[None], (BATCH, N_META, D_MODEL))
    hp = jnp.concatenate([meta, x_prompt, jnp.zeros((BATCH, T_PAD - T_P, D_MODEL), x_prompt.dtype)],
                         axis=1)
    cos_p, sin_p = _rope_tables(jnp.arange(T_PAD, dtype=jnp.int32))
    (qa, qlat, qpe, ka, va, ckv, kpe, kab, vab, ckvb, kpeb) = _project(
        hp, cos_p, sin_p, wts, TM_DENSE, BF16)
    ya, yb = _prompt_attention(qa, qlat, qpe, kab, vab, ckvb, kpeb, bias_tiles, lam_p, subln,
                               wts['w_uv'])
    y_p, conv_p = _ffn(hp, ya, yb, wts, TM_DENSE)

    xs = x_sample.reshape(1, DEC_BATCH, D_MODEL)
    cos_s, sin_s = _rope_tables(jnp.full((DEC_BATCH,), PAST_LEN, jnp.int32))
    (qa_s, qlat_s, qpe_s, ka_s, va_s, ckv_s, kpe_s, _, _, _, _) = _project(
        xs, cos_s, sin_s, wts, DEC_BATCH, F32)
    lane_map = np.arange(2 * D_HEAD_A)[None, :] // D_HEAD_A == (np.arange(2 * H_A) % 2)[:, None]
    per_head_rows = lambda a: jnp.repeat(a.reshape(DEC_BATCH, H_A, 2 * D_HEAD_A), 2, axis=1)
    qd = jnp.where(jnp.asarray(lane_map)[None], per_head_rows(qa_s[0]), 0.0)
    ql = jnp.transpose(qlat_s[:, 0], (1, 0, 2))
    qr = qpe_s[0].reshape(DEC_BATCH, H_B, D_ROPE)
    n_pool = cache_diff_k.shape[1]
    oa_s, ob_s = _paged_attention(
        page_table.reshape(-1), qd, ql, qr, bias_past, bias_self,
        per_head_rows(ka_s[0]), per_head_rows(va_s[0]),
        ckv_s.reshape(DEC_BATCH, 1, KV_LORA), kpe_s.reshape(DEC_BATCH, 1, D_ROPE),
        cache_diff_k.reshape(n_pool * ROWS_PER_PAGE, 2 * D_HEAD_A),
        cache_diff_v.reshape(n_pool * ROWS_PER_PAGE, D_V_A),
        cache_mla_latent[0], jnp.swapaxes(cache_mla_rope[0], 1, 2))
    ya_s, yb_s = _finish(jnp.transpose(oa_s, (1, 0, 2)), jnp.transpose(ob_s, (1, 0, 2)), lam_p, subln,
                         wts['w_uv'])
    state = (state_conv[0, :, 0].reshape(1, DEC_BATCH, D_FF), state_conv[0, :, 1].reshape(1, DEC_BATCH, D_FF))
    y_s, a_s = _ffn(xs, ya_s.reshape(1, DEC_BATCH, -1), yb_s.reshape(1, DEC_BATCH, -1), wts, DEC_BATCH,
                    state=state)

    shape5 = lambda a, n, t: a.reshape(1, n, t, H_A, 2 * D_HEAD_A)
    return (
        y_p[:, N_META:T_P],
        y_s.reshape(DEC_BATCH, 1, D_MODEL),
        shape5(ka[:, :T_P], BATCH, T_P),
        shape5(va[:, :T_P], BATCH, T_P),
        ckv[:, :T_P][None],
        kpe[:, :T_P][None],
        conv_p[None],
        shape5(ka_s, DEC_BATCH, 1),
        shape5(va_s, DEC_BATCH, 1),
        ckv_s.reshape(1, DEC_BATCH, 1, KV_LORA),
        kpe_s.reshape(1, DEC_BATCH, 1, D_ROPE),
        jnp.stack([state_conv[0, :, 1], a_s[0]], axis=1)[None],
    )
```

```python
import functools
import math

import numpy as np
import jax
import jax.numpy as jnp
from jax import lax
from jax.experimental import pallas as pl
from jax.experimental.pallas import tpu as pltpu

F32 = jnp.float32
BF16 = jnp.bfloat16

D_MODEL = 1024
BATCH = 8
SEQ = 2048
DEC_BATCH = 128
PAST_LEN = 16384
PAGE_SIZE = 128
N_PAGES = PAST_LEN // PAGE_SIZE
N_META = 16
H_A = 4
D_HEAD_A = 64
D_V_A = 128
H_B = 8
Q_LORA = 384
KV_LORA = 256
D_NOPE = 64
D_ROPE = 32
D_V_B = 64
ROPE_BASE = 10000.0
N_BUCKETS = 32
MAX_DISTANCE = 128
D_FF = 2816
EPS = 1e-6
NEG = -1e30
SCALE_A = D_HEAD_A ** -0.5
SCALE_B = (D_NOPE + D_ROPE) ** -0.5
LAM_INIT = 0.8 - 0.6 * math.exp(-0.3 * 0)
LOG2E = math.log2(math.e)

T_P = SEQ + N_META
LANES = 128
T_PAD = 18 * LANES
TM_DENSE = T_PAD // 8
TQ = LANES
PAGES_PER_STEP = 8
VMEM_LIMIT = 56 * 1024 * 1024

W_A = H_A * 2 * D_HEAD_A
IN_QL = 3 * W_A
IN_KVL = IN_QL + Q_LORA
IN_KPE = IN_KVL + KV_LORA
IN_COLS = IN_KPE + LANES
Q_NOPE_COLS = H_B * D_NOPE
Q_ROPE_COLS = H_B * D_ROPE


def _rms(x, w):
    return x * lax.rsqrt(jnp.mean(x * x, axis=-1, keepdims=True) + EPS) * w


def _rope_lanes(x, c, s):
    lane = lax.broadcasted_iota(jnp.int32, x.shape, 1)
    from_right = pltpu.roll(x, LANES - 16, 1)
    from_left = pltpu.roll(x, 16, 1)
    swapped = jnp.where((lane & 16) == 0, from_right, from_left)
    return x * c + swapped * s


def _nt_dot(a, b):
    return lax.dot_general(a, b, (((1,), (1,)), ((), ())), preferred_element_type=F32)


def _dot(a, b):
    return jnp.dot(a, b, preferred_element_type=F32)


def _proj_kernel(x_ref, cos_ref, sin_ref, nw_ref, win_ref, qnw_ref, kvnw_ref, wuq_ref, wuk_ref,
                 qa_ref, qlat_ref, qpe_ref, ka_ref, va_ref, ckv_ref, kpe_ref,
                 kab_ref, vab_ref, ckvb_ref, kpeb_ref):
    x = x_ref[...]
    n = _rms(x, nw_ref[...]).astype(BF16)
    h = _dot(n, win_ref[...])
    qa_ref[...] = h[:, 0:W_A].astype(qa_ref.dtype)
    ka = h[:, W_A:2 * W_A]
    va = h[:, 2 * W_A:3 * W_A]
    ka_ref[...] = ka
    va_ref[...] = va
    kab_ref[...] = ka.astype(BF16)
    vab_ref[...] = va.astype(BF16)
    c = cos_ref[...]
    s = sin_ref[...]

    qn = _rms(h[:, IN_QL:IN_KVL], qnw_ref[...]).astype(BF16)
    q = _dot(qn, wuq_ref[...])
    for half in range(2):
        lo = Q_NOPE_COLS + half * LANES
        qpe_ref[:, half * LANES:(half + 1) * LANES] = _rope_lanes(
            q[:, lo:lo + LANES], c[:, half * LANES:(half + 1) * LANES],
            s[:, half * LANES:(half + 1) * LANES]).astype(qpe_ref.dtype)
    for pair in range(H_B // 2):
        qp = q[:, pair * LANES:(pair + 1) * LANES].astype(BF16)
        for head in (2 * pair, 2 * pair + 1):
            qlat_ref[head] = _dot(qp, wuk_ref[head]).astype(qlat_ref.dtype)

    ckv = _rms(h[:, IN_KVL:IN_KPE], kvnw_ref[...])
    ckv_ref[...] = ckv
    ckvb_ref[...] = ckv.astype(BF16)
    kpe4 = _rope_lanes(h[:, IN_KPE:IN_COLS], c[:, 0:LANES], s[:, 0:LANES])
    kpe_ref[...] = kpe4[:, 0:D_ROPE]
    kpeb_ref[...] = kpe4.astype(BF16)


def _project(x3, cos_t, sin_t, wts, tm, q_dtype):
    nb, t, _ = x3.shape
    nt = t // tm
    rows = lambda width: pl.BlockSpec((None, tm, width), lambda b, i: (b, i, 0))
    table = pl.BlockSpec((tm, Q_ROPE_COLS), lambda b, i: (i, 0))
    full = lambda a: pl.BlockSpec(a.shape, lambda b, i: (0,) * a.ndim)
    weights = (wts['norm_mix_pre'], wts['w_in'], wts['q_norm_w'], wts['kv_norm_w'], wts['w_uq'],
               wts['w_uk'])
    out_shape = (
        jax.ShapeDtypeStruct((nb, t, W_A), q_dtype),
        jax.ShapeDtypeStruct((H_B, nb, t, KV_LORA), q_dtype),
        jax.ShapeDtypeStruct((nb, t, Q_ROPE_COLS), q_dtype),
        jax.ShapeDtypeStruct((nb, t, W_A), F32),
        jax.ShapeDtypeStruct((nb, t, W_A), F32),
        jax.ShapeDtypeStruct((nb, t, KV_LORA), F32),
        jax.ShapeDtypeStruct((nb, t, D_ROPE), F32),
        jax.ShapeDtypeStruct((nb, t, W_A), BF16),
        jax.ShapeDtypeStruct((nb, t, W_A), BF16),
        jax.ShapeDtypeStruct((nb, t, KV_LORA), BF16),
        jax.ShapeDtypeStruct((nb, t, LANES), BF16),
    )
    out_specs = (
        rows(W_A),
        pl.BlockSpec((H_B, None, tm, KV_LORA), lambda b, i: (0, b, i, 0)),
        rows(Q_ROPE_COLS), rows(W_A), rows(W_A), rows(KV_LORA), rows(D_ROPE),
        rows(W_A), rows(W_A), rows(KV_LORA), rows(LANES),
    )
    return pl.pallas_call(
        _proj_kernel,
        grid=(nb, nt),
        in_specs=[rows(D_MODEL), table, table] + [full(w) for w in weights],
        out_specs=out_specs,
        out_shape=out_shape,
        compiler_params=pltpu.CompilerParams(
            dimension_semantics=("parallel", "parallel"), vmem_limit_bytes=VMEM_LIMIT),
        name="proj",
    )(x3, cos_t, sin_t, *weights)


def _bias_lookup_kernel(rel_ref, bucket_ref, out_ref):
    bucket = bucket_ref[...]
    for head in range(H_A):
        acc = jnp.zeros(bucket.shape, F32)
        for b in range(N_BUCKETS):
            acc = jnp.where(bucket == b, rel_ref[b, head], acc)
        out_ref[head] = acc


def _bias_lookup(rel_bias, bucket):
    return pl.pallas_call(
        _bias_lookup_kernel,
        in_specs=[pl.BlockSpec(memory_space=pltpu.SMEM),
                  pl.BlockSpec(bucket.shape, lambda: (0, 0))],
        out_specs=pl.BlockSpec((H_A,) + bucket.shape, lambda: (0, 0, 0)),
        out_shape=jax.ShapeDtypeStruct((H_A,) + bucket.shape, F32),
        name="bias_lookup",
    )(rel_bias, bucket)


def _t5_bucket(n):
    n = np.asarray(n, np.int64)
    max_exact = N_BUCKETS // 2
    nf = np.maximum(n, 1).astype(np.float64)
    large = max_exact + (np.log(nf / max_exact) / math.log(MAX_DISTANCE / max_exact)
                         * (N_BUCKETS - max_exact)).astype(np.int64)
    return np.where(n < max_exact, n, np.minimum(large, N_BUCKETS - 1)).astype(np.int32)


def _diff_lambda(lam_ref):
    lp = lam_ref[...]
    s1 = jnp.sum(lp[0:1] * lp[1:2], axis=-1, keepdims=True)
    s2 = jnp.sum(lp[2:3] * lp[3:4], axis=-1, keepdims=True)
    return jnp.exp(s1) - jnp.exp(s2) + LAM_INIT


def _diff_head_finish(o1, o2, lam, subln_w):
    d = o1 - lam * o2
    return _rms(d, subln_w) * (1.0 - LAM_INIT)


def _mla_pair_finish(ob_even, ob_odd, wuv_ref, pair):
    return (_dot(ob_even.astype(BF16), wuv_ref[2 * pair])
            + _dot(ob_odd.astype(BF16), wuv_ref[2 * pair + 1]))


TK = 2 * LANES
N_Q_TILES = T_PAD // TQ
N_REAL_Q_TILES = -(-T_P // TQ)


def _prompt_attn_kernel(qa_ref, qlat_ref, qpe_ref, ka_ref, va_ref, ckv_ref, kpe_ref, bias_ref,
                        lam_ref, subln_ref, wuv_ref, ya_ref, yb_ref,
                        sd_ref, sm_ref, md_ref, mm_ref, accd_ref, ld_ref, accm_ref, lm_ref):
    qi = pl.program_id(1)

    @pl.when(qi >= N_REAL_Q_TILES)
    def _():
        ya_ref[...] = jnp.zeros(ya_ref.shape, ya_ref.dtype)
        yb_ref[...] = jnp.zeros(yb_ref.shape, yb_ref.dtype)

    @pl.when(qi < N_REAL_Q_TILES)
    def _():
        n_it = (qi + 2) // 2
        rows_m = H_B * TQ
        lane = lax.broadcasted_iota(jnp.int32, (TQ, LANES), 1)
        heads = [slice(h * LANES, (h + 1) * LANES) for h in range(H_A)]
        q2 = []
        for hs in heads:
            qh = qa_ref[:, hs] * SCALE_A
            zero = jnp.zeros_like(qh)
            q2.append(jnp.concatenate([jnp.where(lane < D_HEAD_A, qh, zero),
                                       jnp.where(lane >= D_HEAD_A, qh, zero)], axis=0))
        q_lat = qlat_ref[...].reshape(rows_m, KV_LORA)
        group = lane >> 5
        q_pe = jnp.concatenate(
            [jnp.where(group == (head % 4), qpe_ref[:, (head // 4) * LANES:(head // 4 + 1) * LANES],
                       jnp.zeros((TQ, LANES), qpe_ref.dtype)) for head in range(H_B)], axis=0)

        md_ref[...] = jnp.full(md_ref.shape, NEG, F32)
        mm_ref[...] = jnp.full(mm_ref.shape, NEG, F32)

        def scores(j, masked):
            ks = pl.ds(pl.multiple_of(j * TK, TK), TK)
            off = qi * TQ - j * TK
            d0 = jnp.clip(qi - 2 * j, 0, 2)
            d1 = jnp.clip(qi - 2 * j - 1, 0, 2)
            if masked:
                row_d = lax.broadcasted_iota(jnp.int32, (2 * TQ, TK), 0) & (TQ - 1)
                lane_d = lax.broadcasted_iota(jnp.int32, (2 * TQ, TK), 1)
                keep_d = row_d + off >= lane_d
                row_m = lax.broadcasted_iota(jnp.int32, (rows_m, TK), 0) & (TQ - 1)
                lane_m = lax.broadcasted_iota(jnp.int32, (rows_m, TK), 1)
                keep_m = row_m + off >= lane_m
            for h, hs in enumerate(heads):
                b = jnp.concatenate([bias_ref[d0, h], bias_ref[d1, h]], axis=1)
                s = _nt_dot(q2[h], ka_ref[ks, hs]) * LOG2E + jnp.concatenate([b, b], axis=0)
                if masked:
                    s = jnp.where(keep_d, s, NEG)
                sd_ref[h, :, ks] = s
                md_ref[h] = jnp.maximum(md_ref[h], jnp.maximum(s[:, 0:LANES], s[:, LANES:TK]))
            s = (_nt_dot(q_lat, ckv_ref[ks, :]) + _nt_dot(q_pe, kpe_ref[ks, :])) * (SCALE_B * LOG2E)
            if masked:
                s = jnp.where(keep_m, s, NEG)
            sm_ref[:, ks] = s
            mm_ref[...] = jnp.maximum(mm_ref[...], jnp.maximum(s[:, 0:LANES], s[:, LANES:TK]))

        def scores_body(j, carry):
            scores(j, False)
            return carry

        lax.fori_loop(0, n_it - 1, scores_body, 0)
        scores(n_it - 1, True)

        m_d = [jnp.broadcast_to(jnp.max(md_ref[h], axis=-1, keepdims=True), (2 * TQ, LANES))
               for h in range(H_A)]
        m_m = jnp.broadcast_to(jnp.max(mm_ref[...], axis=-1, keepdims=True), (rows_m, LANES))
        accd_ref[...] = jnp.zeros(accd_ref.shape, F32)
        ld_ref[...] = jnp.zeros(ld_ref.shape, F32)
        accm_ref[...] = jnp.zeros(accm_ref.shape, F32)
        lm_ref[...] = jnp.zeros(lm_ref.shape, F32)

        def values_body(j, carry):
            ks = pl.ds(pl.multiple_of(j * TK, TK), TK)
            for h, hs in enumerate(heads):
                p = jnp.exp2(sd_ref[h, :, ks] - jnp.concatenate([m_d[h], m_d[h]], axis=1))
                ld_ref[h] = ld_ref[h] + (p[:, 0:LANES] + p[:, LANES:TK])
                accd_ref[h] = accd_ref[h] + _dot(p.astype(BF16), va_ref[ks, hs])
            p = jnp.exp2(sm_ref[:, ks] - jnp.concatenate([m_m, m_m], axis=1))
            lm_ref[...] = lm_ref[...] + (p[:, 0:LANES] + p[:, LANES:TK])
            accm_ref[...] = accm_ref[...] + _dot(p.astype(BF16), ckv_ref[ks, :])
            return carry

        lax.fori_loop(0, n_it, values_body, 0)

        lam = _diff_lambda(lam_ref)
        subln_w = subln_ref[...]
        for h, hs in enumerate(heads):
            o = accd_ref[h] / jnp.sum(ld_ref[h], axis=-1, keepdims=True)
            ya_ref[:, hs] = _diff_head_finish(o[0:TQ], o[TQ:2 * TQ], lam, subln_w).astype(ya_ref.dtype)
        ob = accm_ref[...] / jnp.sum(lm_ref[...], axis=-1, keepdims=True)
        for pair in range(H_B // 2):
            e0 = 2 * pair * TQ
            yb_ref[:, pair * LANES:(pair + 1) * LANES] = _mla_pair_finish(
                ob[e0:e0 + TQ], ob[e0 + TQ:e0 + 2 * TQ], wuv_ref, pair).astype(yb_ref.dtype)


def _prompt_attention(qa, qlat, qpe, kab, vab, ckvb, kpeb, bias_tiles, lam_p, subln_w, wuv):
    nb = qa.shape[0]
    qrows = lambda width: pl.BlockSpec((None, TQ, width), lambda b, i: (b, i, 0))
    keys = lambda width: pl.BlockSpec((None, T_PAD, width), lambda b, i: (b, 0, 0))
    full = lambda a: pl.BlockSpec(a.shape, lambda b, i: (0,) * a.ndim)
    return pl.pallas_call(
        _prompt_attn_kernel,
        grid=(nb, N_Q_TILES),
        in_specs=[qrows(W_A),
                  pl.BlockSpec((H_B, None, TQ, KV_LORA), lambda b, i: (0, b, i, 0)),
                  qrows(Q_ROPE_COLS),
                  keys(W_A), keys(W_A), keys(KV_LORA), keys(LANES),
                  full(bias_tiles), full(lam_p), full(subln_w), full(wuv)],
        out_specs=(qrows(H_A * D_V_A), qrows(H_B * D_V_B)),
        out_shape=(jax.ShapeDtypeStruct((nb, T_PAD, H_A * D_V_A), BF16),
                   jax.ShapeDtypeStruct((nb, T_PAD, H_B * D_V_B), BF16)),
        scratch_shapes=[pltpu.VMEM((H_A, 2 * TQ, T_PAD), F32),
                        pltpu.VMEM((H_B * TQ, T_PAD), F32),
                        pltpu.VMEM((H_A, 2 * TQ, LANES), F32),
                        pltpu.VMEM((H_B * TQ, LANES), F32),
                        pltpu.VMEM((H_A, 2 * TQ, D_V_A), F32),
                        pltpu.VMEM((H_A, 2 * TQ, LANES), F32),
                        pltpu.VMEM((H_B * TQ, KV_LORA), F32),
                        pltpu.VMEM((H_B * TQ, LANES), F32)],
        compiler_params=pltpu.CompilerParams(
            dimension_semantics=("parallel", "parallel"), vmem_limit_bytes=VMEM_LIMIT),
        name="prompt_attn",
    )(qa, qlat, qpe, kab, vab, ckvb, kpeb, bias_tiles, lam_p, subln_w, wuv)


ROWS_PER_PAGE = PAGE_SIZE * H_A


def _online_update(s, m_ref, l_ref):
    m = m_ref[...]
    m_new = jnp.maximum(m, jnp.max(s, axis=-1, keepdims=True))
    p = jnp.exp(s - m_new)
    corr = jnp.exp(m - m_new)
    l_ref[...] = l_ref[...] * corr + jnp.sum(p, axis=-1, keepdims=True)
    m_ref[...] = m_new
    return p, corr


def _paged_attn_kernel(pt_ref, qd_ref, ql_ref, qr_ref, bias_ref, bself_ref,
                       knew_ref, vnew_ref, cnew_ref, rnew_ref, ck_hbm, cv_hbm, cc_hbm, cr_hbm,
                       oa_ref, ob_ref, kbuf, vbuf, cbuf, rbuf, sem,
                       md_ref, ld_ref, mm_ref, lm_ref, acca_ref, accb_ref):
    g = PAGES_PER_STEP
    step = pl.program_id(1)
    n_steps = pl.num_programs(1)
    t = pl.program_id(0) * n_steps + step
    total = pl.num_programs(0) * n_steps
    slot = t & 1

    def page_copies(t_src, dst):
        copies = []
        for j in range(g):
            page = pt_ref[t_src * g + j]
            rows = pl.ds(pl.multiple_of(page * ROWS_PER_PAGE, ROWS_PER_PAGE), ROWS_PER_PAGE)
            copies += [pltpu.make_async_copy(ck_hbm.at[rows], kbuf.at[dst, j], sem.at[dst]),
                       pltpu.make_async_copy(cv_hbm.at[rows], vbuf.at[dst, j], sem.at[dst]),
                       pltpu.make_async_copy(cc_hbm.at[page], cbuf.at[dst, j], sem.at[dst]),
                       pltpu.make_async_copy(cr_hbm.at[page], rbuf.at[dst, j], sem.at[dst])]
        return copies

    @pl.when(t == 0)
    def _():
        for copy in page_copies(0, 0):
            copy.start()

    @pl.when(t + 1 < total)
    def _():
        for copy in page_copies(t + 1, 1 - slot):
            copy.start()

    @pl.when(step == 0)
    def _():
        md_ref[...] = jnp.full(md_ref.shape, NEG, F32)
        mm_ref[...] = jnp.full(mm_ref.shape, NEG, F32)
        ld_ref[...] = jnp.zeros(ld_ref.shape, F32)
        lm_ref[...] = jnp.zeros(lm_ref.shape, F32)
        acca_ref[...] = jnp.zeros(acca_ref.shape, F32)
        accb_ref[...] = jnp.zeros(accb_ref.shape, F32)

    qd = qd_ref[...]
    ql = ql_ref[...]
    qr = qr_ref[...]
    wd = g * ROWS_PER_PAGE
    bias = bias_ref[:, pl.ds(pl.multiple_of(step * wd, wd), wd)]

    for copy in page_copies(t, slot):
        copy.wait()

    sd = jnp.concatenate([_nt_dot(qd, kbuf[slot, j]) for j in range(g)], axis=1)
    sd = sd * SCALE_A + bias
    sm = jnp.concatenate([_nt_dot(ql, cbuf[slot, j]) + _dot(qr, rbuf[slot, j])
                          for j in range(g)], axis=1) * SCALE_B
    pd, corr_d = _online_update(sd, md_ref, ld_ref)
    pm, corr_m = _online_update(sm, mm_ref, lm_ref)
    pa = _dot(pd[:, 0:ROWS_PER_PAGE], vbuf[slot, 0])
    pb = _dot(pm[:, 0:PAGE_SIZE], cbuf[slot, 0])
    for j in range(1, g):
        pa = pa + _dot(pd[:, j * ROWS_PER_PAGE:(j + 1) * ROWS_PER_PAGE], vbuf[slot, j])
        pb = pb + _dot(pm[:, j * PAGE_SIZE:(j + 1) * PAGE_SIZE], cbuf[slot, j])
    acca_ref[...] = acca_ref[...] * corr_d + pa
    accb_ref[...] = accb_ref[...] * corr_m + pb

    @pl.when(step == pl.num_programs(1) - 1)
    def _():
        s_d = jnp.sum(qd * knew_ref[...], axis=-1, keepdims=True) * SCALE_A + bself_ref[...]
        s_m = (jnp.sum(ql * cnew_ref[...], axis=-1, keepdims=True)
               + jnp.sum(qr * rnew_ref[...], axis=-1, keepdims=True)) * SCALE_B
        p_d, c_d = _online_update(s_d, md_ref, ld_ref)
        p_m, c_m = _online_update(s_m, mm_ref, lm_ref)
        oa_ref[...] = (acca_ref[...] * c_d + p_d * vnew_ref[...]) / ld_ref[...]
        ob_ref[...] = (accb_ref[...] * c_m + p_m * cnew_ref[...]) / lm_ref[...]


def _paged_attention(page_table, qd, ql, qr, bias, bias_self, k_new, v_new, c_new, r_new,
                     cache_k, cache_v, cache_c, cache_r):
    g = PAGES_PER_STEP
    n_steps = N_PAGES // g
    half = H_A * 2
    per_seq = lambda a: pl.BlockSpec((None,) + a.shape[1:], lambda b, s, pt: (b,) + (0,) * (a.ndim - 1))
    full = lambda a: pl.BlockSpec(a.shape, lambda b, s, pt: (0,) * a.ndim)
    hbm = pl.BlockSpec(memory_space=pl.ANY)
    grid_spec = pltpu.PrefetchScalarGridSpec(
        num_scalar_prefetch=1,
        grid=(DEC_BATCH, n_steps),
        in_specs=[per_seq(qd), per_seq(ql), per_seq(qr), full(bias), full(bias_self),
                  per_seq(k_new), per_seq(v_new), per_seq(c_new), per_seq(r_new),
                  hbm, hbm, hbm, hbm],
        out_specs=(pl.BlockSpec((None, half, D_V_A), lambda b, s, pt: (b, 0, 0)),
                   pl.BlockSpec((None, H_B, KV_LORA), lambda b, s, pt: (b, 0, 0))),
        scratch_shapes=[pltpu.VMEM((2, g, ROWS_PER_PAGE, 2 * D_HEAD_A), F32),
                        pltpu.VMEM((2, g, ROWS_PER_PAGE, D_V_A), F32),
                        pltpu.VMEM((2, g, PAGE_SIZE, KV_LORA), F32),
                        pltpu.VMEM((2, g, D_ROPE, PAGE_SIZE), F32),
                        pltpu.SemaphoreType.DMA((2,)),
                        pltpu.VMEM((half, 1), F32), pltpu.VMEM((half, 1), F32),
                        pltpu.VMEM((H_B, 1), F32), pltpu.VMEM((H_B, 1), F32),
                        pltpu.VMEM((half, D_V_A), F32), pltpu.VMEM((H_B, KV_LORA), F32)],
    )
    return pl.pallas_call(
        _paged_attn_kernel,
        grid_spec=grid_spec,
        out_shape=(jax.ShapeDtypeStruct((DEC_BATCH, half, D_V_A), F32),
                   jax.ShapeDtypeStruct((DEC_BATCH, H_B, KV_LORA), F32)),
        compiler_params=pltpu.CompilerParams(
            dimension_semantics=("arbitrary", "arbitrary"), vmem_limit_bytes=VMEM_LIMIT),
        name="paged_attn",
    )(page_table, qd, ql, qr, bias, bias_self, k_new, v_new, c_new, r_new,
      cache_k, cache_v, cache_c, cache_r)


def _finish_kernel(oa_ref, ob_ref, lam_ref, subln_ref, wuv_ref, ya_ref, yb_ref):
    lam = _diff_lambda(lam_ref)
    subln_w = subln_ref[...]
    for head in range(H_A):
        ya_ref[:, head * D_V_A:(head + 1) * D_V_A] = _diff_head_finish(
            oa_ref[2 * head], oa_ref[2 * head + 1], lam, subln_w).astype(ya_ref.dtype)
    for pair in range(H_B // 2):
        yb_ref[:, pair * LANES:(pair + 1) * LANES] = _mla_pair_finish(
            ob_ref[2 * pair], ob_ref[2 * pair + 1], wuv_ref, pair).astype(yb_ref.dtype)


def _finish(oa_t, ob_t, lam_p, subln_w, wuv):
    n = oa_t.shape[1]
    full = lambda a: pl.BlockSpec(a.shape, lambda: (0,) * a.ndim)
    return pl.pallas_call(
        _finish_kernel,
        in_specs=[full(oa_t), full(ob_t), full(lam_p), full(subln_w), full(wuv)],
        out_specs=(pl.BlockSpec((n, H_A * D_V_A), lambda: (0, 0)),
                   pl.BlockSpec((n, H_B * D_V_B), lambda: (0, 0))),
        out_shape=(jax.ShapeDtypeStruct((n, H_A * D_V_A), BF16),
                   jax.ShapeDtypeStruct((n, H_B * D_V_B), BF16)),
        name="sample_finish",
    )(oa_t, ob_t, lam_p, subln_w, wuv)


def _ffn_kernel(sample, *refs):
    if sample:
        (x_ref, ya_ref, yb_ref, s0_ref, s1_ref, nw_pre, wgate, woa, wob, wo, nw_post, nw_fpre, wup,
         convw, convb, wdown, nw_fpost, y_ref, aux_ref) = refs
    else:
        (x_ref, ya_ref, yb_ref, nw_pre, wgate, woa, wob, wo, nw_post, nw_fpre, wup,
         convw, convb, wdown, nw_fpost, y_ref, aux_ref, carry_ref) = refs
    x = x_ref[...]
    tm = x.shape[0]
    n = _rms(x, nw_pre[...]).astype(BF16)
    gate = jax.nn.sigmoid(_dot(n, wgate[...]))
    merged = (gate[:, 0:D_MODEL] * _dot(ya_ref[...], woa[...])
              + gate[:, D_MODEL:2 * D_MODEL] * _dot(yb_ref[...], wob[...]))
    mix = _dot(merged.astype(BF16), wo[...])
    h = x + _rms(mix, nw_post[...])
    n2 = _rms(h, nw_fpre[...]).astype(BF16)
    up = _dot(n2, wup[...])
    a = up[:, 0:D_FF]
    b = up[:, D_FF:2 * D_FF]
    cw = convw[...]
    if sample:
        prev2 = s0_ref[...]
        prev1 = s1_ref[...]
        aux_ref[...] = a
    else:
        ti = pl.program_id(1)

        @pl.when(ti == 0)
        def _():
            carry_ref[...] = jnp.zeros(carry_ref.shape, F32)

        row = lax.broadcasted_iota(jnp.int32, (tm, 1), 0)
        pos = ti * tm + row
        carry = carry_ref[...]
        prev1 = jnp.where(row == 0, carry[7:8], pltpu.roll(a, 1, 0))
        prev2 = jnp.where(row == 0, carry[6:7], jnp.where(row == 1, carry[7:8], pltpu.roll(a, 2, 0)))
        prev1 = jnp.where(pos >= 1, prev1, 0.0)
        prev2 = jnp.where(pos >= 2, prev2, 0.0)
        carry_ref[...] = a[tm - 8:tm]
        last = T_P - 1 - (T_PAD - tm)

        @pl.when(ti == pl.num_programs(1) - 1)
        def _():
            aux_ref[...] = a[last - 1:last + 1]

    conv = convb[...] + cw[0:1] * prev2
    conv = conv + cw[1:2] * prev1
    conv = conv + cw[2:3] * a
    hh = jax.nn.gelu(conv) * b
    f = _dot(hh.astype(BF16), wdown[...])
    y_ref[...] = h + _rms(f, nw_fpost[...])


def _ffn(x3, ya3, yb3, wts, tm, state=None):
    nb, t, _ = x3.shape
    nt = t // tm
    sample = state is not None
    rows = lambda width: pl.BlockSpec((None, tm, width), lambda b, i: (b, i, 0))
    full = lambda a: pl.BlockSpec(a.shape, lambda b, i: (0,) * a.ndim,
                                  pipeline_mode=pl.Buffered(1))
    weights = (wts['norm_mix_pre'], wts['w_gate'], wts['w_oa'], wts['w_ob'], wts['w_o'],
               wts['norm_mix_post'], wts['norm_ffn_pre'], wts['w_up'], wts['conv_w'], wts['conv_b'],
               wts['w_down'], wts['norm_ffn_post'])
    acts = [x3, ya3, yb3]
    act_specs = [rows(D_MODEL), rows(H_A * D_V_A), rows(H_B * D_V_B)]
    if sample:
        acts += [state[0], state[1]]
        act_specs += [rows(D_FF), rows(D_FF)]
        aux_shape = jax.ShapeDtypeStruct((nb, t, D_FF), F32)
        aux_spec = rows(D_FF)
        scratch = []
    else:
        aux_shape = jax.ShapeDtypeStruct((nb, 2, D_FF), F32)
        aux_spec = pl.BlockSpec((None, 2, D_FF), lambda b, i: (b, 0, 0))
        scratch = [pltpu.VMEM((8, D_FF), F32)]
    return pl.pallas_call(
        functools.partial(_ffn_kernel, sample),
        grid=(nb, nt),
        in_specs=act_specs + [full(w) for w in weights],
        out_specs=(rows(D_MODEL), aux_spec),
        out_shape=(jax.ShapeDtypeStruct((nb, t, D_MODEL), F32), aux_shape),
        scratch_shapes=scratch,
        compiler_params=pltpu.CompilerParams(
            dimension_semantics=("parallel", "arbitrary"), vmem_limit_bytes=VMEM_LIMIT),
        name="ffn_sample" if sample else "ffn_prompt",
    )(*acts, *weights)


def _rope_tables(pos):
    half = D_ROPE // 2
    freqs = ROPE_BASE ** (-jnp.arange(half, dtype=F32) / half)
    ang = pos.astype(F32)[:, None] * freqs
    cos, sin = jnp.cos(ang), jnp.sin(ang)
    return (jnp.tile(jnp.concatenate([cos, cos], axis=-1), (1, H_B)),
            jnp.tile(jnp.concatenate([-sin, sin], axis=-1), (1, H_B)))


def _prepare_weights(norm_mix_pre, norm_mix_post, norm_ffn_pre, norm_ffn_post, w_in, w_gate, q_norm_w,
                     kv_norm_w, w_uq, w_ukv, w_oa, w_ob, w_o, w_up, conv_w, conv_b, w_down):
    row = lambda v: v[0].reshape(1, -1)
    w_in0 = w_in[0]
    w_in_aug = jnp.concatenate([w_in0] + [w_in0[:, IN_KPE:]] * 3, axis=1).astype(BF16)
    w_uq3 = w_uq[0].reshape(Q_LORA, H_B, D_NOPE + D_ROPE)
    w_uq_perm = jnp.concatenate([w_uq3[:, :, :D_NOPE].reshape(Q_LORA, Q_NOPE_COLS),
                                 w_uq3[:, :, D_NOPE:].reshape(Q_LORA, Q_ROPE_COLS)], axis=1).astype(BF16)
    w_ukv0 = w_ukv[0]
    uk = jnp.transpose(w_ukv0[:, :, :D_NOPE], (1, 2, 0))
    uv = jnp.transpose(w_ukv0[:, :, D_NOPE:], (1, 0, 2))
    zk = jnp.zeros_like(uk)
    zv = jnp.zeros_like(uv)
    even = (jnp.arange(H_B) % 2 == 0)
    w_uk = jnp.where(even[:, None, None], jnp.concatenate([uk, zk], axis=1),
                     jnp.concatenate([zk, uk], axis=1)).astype(BF16)
    w_uv = jnp.where(even[:, None, None], jnp.concatenate([uv, zv], axis=2),
                     jnp.concatenate([zv, uv], axis=2)).astype(BF16)
    return dict(
        norm_mix_pre=row(norm_mix_pre), norm_mix_post=row(norm_mix_post),
        norm_ffn_pre=row(norm_ffn_pre), norm_ffn_post=row(norm_ffn_post),
        q_norm_w=row(q_norm_w), kv_norm_w=row(kv_norm_w),
        w_in=w_in_aug, w_uq=w_uq_perm, w_uk=w_uk, w_uv=w_uv,
        w_gate=w_gate[0].astype(BF16), w_oa=w_oa[0].astype(BF16), w_ob=w_ob[0].astype(BF16),
        w_o=w_o[0].astype(BF16), w_up=w_up[0].astype(BF16), w_down=w_down[0].astype(BF16),
        conv_w=conv_w[0], conv_b=row(conv_b))


def kernel(x_prompt, x_sample, cache_diff_k, cache_diff_v, cache_mla_latent, cache_mla_rope, state_conv, page_table, meta_tokens, rel_bias, norm_mix_pre, norm_mix_post, norm_ffn_pre, norm_ffn_post, w_in, w_gate, lambda_q1, lambda_k1, lambda_q2, lambda_k2, subln_w, q_norm_w, kv_norm_w, w_uq, w_ukv, w_oa, w_ob, w_o, w_up, conv_w, conv_b, w_down):
    assert x_prompt.shape == (BATCH, SEQ, D_MODEL) and x_sample.shape == (DEC_BATCH, 1, D_MODEL)
    assert page_table.shape == (DEC_BATCH, N_PAGES) and w_in.shape[0] == 1
    wts = _prepare_weights(norm_mix_pre, norm_mix_post, norm_ffn_pre, norm_ffn_post, w_in, w_gate,
                           q_norm_w, kv_norm_w, w_uq, w_ukv, w_oa, w_ob, w_o, w_up, conv_w, conv_b,
                           w_down)
    lam_p = jnp.concatenate([lambda_q1, lambda_k1, lambda_q2, lambda_k2], axis=0)
    subln = subln_w[0].reshape(1, D_V_A)

    i = np.arange(TQ)
    tile_dist = np.stack([np.maximum(d * TQ + i[:, None] - i[None, :], 0) for d in range(3)])
    bias_tiles = _bias_lookup(rel_bias, jnp.asarray(_t5_bucket(tile_dist).reshape(3 * TQ, TQ)))
    bias_tiles = bias_tiles.reshape(H_A, 3, TQ, TQ).transpose(1, 0, 2, 3) * LOG2E
    past_dist = PAST_LEN - np.arange(PAST_LEN + PAGE_SIZE)
    past_dist = np.where(past_dist >= 0, past_dist, 0)
    past_bias = _bias_lookup(rel_bias, jnp.asarray(_t5_bucket(past_dist).reshape(N_PAGES + 1, PAGE_SIZE)))
    past_bias = past_bias.reshape(H_A, PAST_LEN + PAGE_SIZE)
    same_head = jnp.asarray(np.eye(H_A, dtype=bool))
    bias_past = jnp.where(same_head[:, None, :], past_bias[:, :PAST_LEN, None], NEG)
    bias_past = jnp.repeat(bias_past.reshape(H_A, PAST_LEN * H_A), 2, axis=0)
    bias_self = jnp.repeat(past_bias[:, PAST_LEN:PAST_LEN + 1], 2, axis=0)

    meta = jnp.broadcast_to(meta_tokens.astype(x_prompt.dtype)[None], (BATCH, N_META, D_MODEL))
    hp = jnp.concatenate([meta, x_prompt, jnp.zeros((BATCH, T_PAD - T_P, D_MODEL), x_prompt.dtype)],
                         axis=1)
    cos_p, sin_p = _rope_tables(jnp.arange(T_PAD, dtype=jnp.int32))
    (qa, qlat, qpe, ka, va, ckv, kpe, kab, vab, ckvb, kpeb) = _project(
        hp, cos_p, sin_p, wts, TM_DENSE, BF16)
    ya, yb = _prompt_attention(qa, qlat, qpe, kab, vab, ckvb, kpeb, bias_tiles, lam_p, subln,
                               wts['w_uv'])
    y_p, conv_p = _ffn(hp, ya, yb, wts, TM_DENSE)

    xs = x_sample.reshape(1, DEC_BATCH, D_MODEL)
    cos_s, sin_s = _rope_tables(jnp.full((DEC_BATCH,), PAST_LEN, jnp.int32))
    (qa_s, qlat_s, qpe_s, ka_s, va_s, ckv_s, kpe_s, _, _, _, _) = _project(
        xs, cos_s, sin_s, wts, DEC_BATCH, F32)
    lane_map = np.arange(2 * D_HEAD_A)[None, :] // D_HEAD_A == (np.arange(2 * H_A) % 2)[:, None]
    per_head_rows = lambda a: jnp.repeat(a.reshape(DEC_BATCH, H_A, 2 * D_HEAD_A), 2, axis=1)
    qd = jnp.where(jnp.asarray(lane_map)[None], per_head_rows(qa_s[0]), 0.0)
    ql = jnp.transpose(qlat_s[:, 0], (1, 0, 2))
    qr = qpe_s[0].reshape(DEC_BATCH, H_B, D_ROPE)
    n_pool = cache_diff_k.shape[1]
    oa_s, ob_s = _paged_attention(
        page_table.reshape(-1), qd, ql, qr, bias_past, bias_self,
        per_head_rows(ka_s[0]), per_head_rows(va_s[0]),
        ckv_s.reshape(DEC_BATCH, 1, KV_LORA), kpe_s.reshape(DEC_BATCH, 1, D_ROPE),
        cache_diff_k.reshape(n_pool * ROWS_PER_PAGE, 2 * D_HEAD_A),
        cache_diff_v.reshape(n_pool * ROWS_PER_PAGE, D_V_A),
        cache_mla_latent[0], jnp.swapaxes(cache_mla_rope[0], 1, 2))
    ya_s, yb_s = _finish(jnp.transpose(oa_s, (1, 0, 2)), jnp.transpose(ob_s, (1, 0, 2)), lam_p, subln,
                         wts['w_uv'])
    state = (state_conv[0, :, 0].reshape(1, DEC_BATCH, D_FF), state_conv[0, :, 1].reshape(1, DEC_BATCH, D_FF))
    y_s, a_s = _ffn(xs, ya_s.reshape(1, DEC_BATCH, -1), yb_s.reshape(1, DEC_BATCH, -1), wts, DEC_BATCH,
                    state=state)

    shape5 = lambda a, n, t: a.reshape(1, n, t, H_A, 2 * D_HEAD_A)
    return (
        y_p[:, N_META:T_P],
        y_s.reshape(DEC_BATCH, 1, D_MODEL),
        shape5(ka[:, :T_P], BATCH, T_P),
        shape5(va[:, :T_P], BATCH, T_P),
        ckv[:, :T_P][None],
        kpe[:, :T_P][None],
        conv_p[None],
        shape5(ka_s, DEC_BATCH, 1),
        shape5(va_s, DEC_BATCH, 1),
        ckv_s.reshape(1, DEC_BATCH, 1, KV_LORA),
        kpe_s.reshape(1, DEC_BATCH, 1, D_ROPE),
        jnp.stack([state_conv[0, :, 1], a_s[0]], axis=1)[None],
    )
```

```python
import functools
import math

import numpy as np
import jax
import jax.numpy as jnp
from jax import lax
from jax.experimental import pallas as pl
from jax.experimental.pallas import tpu as pltpu

F32 = jnp.float32
BF16 = jnp.bfloat16

D_MODEL = 1024
BATCH = 8
SEQ = 2048
DEC_BATCH = 128
PAST_LEN = 16384
PAGE_SIZE = 128
N_PAGES = PAST_LEN // PAGE_SIZE
N_META = 16
H_A = 4
D_HEAD_A = 64
D_V_A = 128
H_B = 8
Q_LORA = 384
KV_LORA = 256
D_NOPE = 64
D_ROPE = 32
D_V_B = 64
ROPE_BASE = 10000.0
N_BUCKETS = 32
MAX_DISTANCE = 128
D_FF = 2816
EPS = 1e-6
NEG = -1e30
SCALE_A = D_HEAD_A ** -0.5
SCALE_B = (D_NOPE + D_ROPE) ** -0.5
LAM_INIT = 0.8 - 0.6 * math.exp(-0.3 * 0)
LOG2E = math.log2(math.e)

T_P = SEQ + N_META
LANES = 128
T_PAD = 18 * LANES
TM_DENSE = T_PAD // 8
TQ = LANES
PAGES_PER_STEP = 8
VMEM_LIMIT = 56 * 1024 * 1024

W_A = H_A * 2 * D_HEAD_A
IN_QL = 3 * W_A
IN_KVL = IN_QL + Q_LORA
IN_KPE = IN_KVL + KV_LORA
IN_COLS = IN_KPE + LANES
Q_NOPE_COLS = H_B * D_NOPE
Q_ROPE_COLS = H_B * D_ROPE


def _rms(x, w):
    return x * lax.rsqrt(jnp.mean(x * x, axis=-1, keepdims=True) + EPS) * w


def _rope_lanes(x, c, s):
    lane = lax.broadcasted_iota(jnp.int32, x.shape, 1)
    from_right = pltpu.roll(x, LANES - 16, 1)
    from_left = pltpu.roll(x, 16, 1)
    swapped = jnp.where((lane & 16) == 0, from_right, from_left)
    return x * c + swapped * s


def _nt_dot(a, b):
    return lax.dot_general(a, b, (((1,), (1,)), ((), ())), preferred_element_type=F32)


def _dot(a, b):
    return jnp.dot(a, b, preferred_element_type=F32)


def _proj_kernel(x_ref, cos_ref, sin_ref, nw_ref, win_ref, qnw_ref, kvnw_ref, wuq_ref, wuk_ref,
                 qa_ref, qlat_ref, qpe_ref, ka_ref, va_ref, ckv_ref, kpe_ref,
                 kab_ref, vab_ref, ckvb_ref, kpeb_ref):
    x = x_ref[...]
    n = _rms(x, nw_ref[...]).astype(BF16)
    h = _dot(n, win_ref[...])
    qa_ref[...] = h[:, 0:W_A].astype(qa_ref.dtype)
    ka = h[:, W_A:2 * W_A]
    va = h[:, 2 * W_A:3 * W_A]
    ka_ref[...] = ka
    va_ref[...] = va
    kab_ref[...] = ka.astype(BF16)
    vab_ref[...] = va.astype(BF16)
    c = cos_ref[...]
    s = sin_ref[...]

    qn = _rms(h[:, IN_QL:IN_KVL], qnw_ref[...]).astype(BF16)
    q = _dot(qn, wuq_ref[...])
    for half in range(2):
        lo = Q_NOPE_COLS + half * LANES
        qpe_ref[:, half * LANES:(half + 1) * LANES] = _rope_lanes(
            q[:, lo:lo + LANES], c[:, half * LANES:(half + 1) * LANES],
            s[:, half * LANES:(half + 1) * LANES]).astype(qpe_ref.dtype)
    for pair in range(H_B // 2):
        qp = q[:, pair * LANES:(pair + 1) * LANES].astype(BF16)
        for head in (2 * pair, 2 * pair + 1):
            qlat_ref[head] = _dot(qp, wuk_ref[head]).astype(qlat_ref.dtype)

    ckv = _rms(h[:, IN_KVL:IN_KPE], kvnw_ref[...])
    ckv_ref[...] = ckv
    ckvb_ref[...] = ckv.astype(BF16)
    kpe4 = _rope_lanes(h[:, IN_KPE:IN_COLS], c[:, 0:LANES], s[:, 0:LANES])
    kpe_ref[...] = kpe4[:, 0:D_ROPE]
    kpeb_ref[...] = kpe4.astype(BF16)


def _project(x3, cos_t, sin_t, wts, tm, q_dtype):
    nb, t, _ = x3.shape
    nt = t // tm
    rows = lambda width: pl.BlockSpec((None, tm, width), lambda b, i: (b, i, 0))
    table = pl.BlockSpec((tm, Q_ROPE_COLS), lambda b, i: (i, 0))
    full = lambda a: pl.BlockSpec(a.shape, lambda b, i: (0,) * a.ndim)
    weights = (wts['norm_mix_pre'], wts['w_in'], wts['q_norm_w'], wts['kv_norm_w'], wts['w_uq'],
               wts['w_uk'])
    out_shape = (
        jax.ShapeDtypeStruct((nb, t, W_A), q_dtype),
        jax.ShapeDtypeStruct((H_B, nb, t, KV_LORA), q_dtype),
        jax.ShapeDtypeStruct((nb, t, Q_ROPE_COLS), q_dtype),
        jax.ShapeDtypeStruct((nb, t, W_A), F32),
        jax.ShapeDtypeStruct((nb, t, W_A), F32),
        jax.ShapeDtypeStruct((nb, t, KV_LORA), F32),
        jax.ShapeDtypeStruct((nb, t, D_ROPE), F32),
        jax.ShapeDtypeStruct((nb, t, W_A), BF16),
        jax.ShapeDtypeStruct((nb, t, W_A), BF16),
        jax.ShapeDtypeStruct((nb, t, KV_LORA), BF16),
        jax.ShapeDtypeStruct((nb, t, LANES), BF16),
    )
    out_specs = (
        rows(W_A),
        pl.BlockSpec((H_B, None, tm, KV_LORA), lambda b, i: (0, b, i, 0)),
        rows(Q_ROPE_COLS), rows(W_A), rows(W_A), rows(KV_LORA), rows(D_ROPE),
        rows(W_A), rows(W_A), rows(KV_LORA), rows(LANES),
    )
    return pl.pallas_call(
        _proj_kernel,
        grid=(nb, nt),
        in_specs=[rows(D_MODEL), table, table] + [full(w) for w in weights],
        out_specs=out_specs,
        out_shape=out_shape,
        compiler_params=pltpu.CompilerParams(
            dimension_semantics=("parallel", "parallel"), vmem_limit_bytes=VMEM_LIMIT),
        name="proj",
    )(x3, cos_t, sin_t, *weights)


def _bias_lookup_kernel(rel_ref, bucket_ref, out_ref):
    bucket = bucket_ref[...]
    for head in range(H_A):
        acc = jnp.zeros(bucket.shape, F32)
        for b in range(N_BUCKETS):
            acc = jnp.where(bucket == b, rel_ref[b, head], acc)
        out_ref[head] = acc


def _bias_lookup(rel_bias, bucket):
    return pl.pallas_call(
        _bias_lookup_kernel,
        in_specs=[pl.BlockSpec(memory_space=pltpu.SMEM),
                  pl.BlockSpec(bucket.shape, lambda: (0, 0))],
        out_specs=pl.BlockSpec((H_A,) + bucket.shape, lambda: (0, 0, 0)),
        out_shape=jax.ShapeDtypeStruct((H_A,) + bucket.shape, F32),
        name="bias_lookup",
    )(rel_bias, bucket)


def _t5_bucket(n):
    n = np.asarray(n, np.int64)
    max_exact = N_BUCKETS // 2
    nf = np.maximum(n, 1).astype(np.float64)
    large = max_exact + (np.log(nf / max_exact) / math.log(MAX_DISTANCE / max_exact)
                         * (N_BUCKETS - max_exact)).astype(np.int64)
    return np.where(n < max_exact, n, np.minimum(large, N_BUCKETS - 1)).astype(np.int32)


def _diff_lambda(lam_ref):
    lp = lam_ref[...]
    s1 = jnp.sum(lp[0:1] * lp[1:2], axis=-1, keepdims=True)
    s2 = jnp.sum(lp[2:3] * lp[3:4], axis=-1, keepdims=True)
    return jnp.exp(s1) - jnp.exp(s2) + LAM_INIT


def _diff_head_finish(o1, o2, lam, subln_w):
    d = o1 - lam * o2
    return _rms(d, subln_w) * (1.0 - LAM_INIT)


def _mla_pair_finish(ob_even, ob_odd, wuv_ref, pair):
    return (_dot(ob_even.astype(BF16), wuv_ref[2 * pair])
            + _dot(ob_odd.astype(BF16), wuv_ref[2 * pair + 1]))


TK = 2 * LANES
N_Q_TILES = T_PAD // TQ
N_REAL_Q_TILES = -(-T_P // TQ)


def _prompt_attn_kernel(qa_ref, qlat_ref, qpe_ref, ka_ref, va_ref, ckv_ref, kpe_ref, bias_ref,
                        lam_ref, subln_ref, wuv_ref, ya_ref, yb_ref,
                        sd_ref, sm_ref, md_ref, mm_ref, accd_ref, ld_ref, accm_ref, lm_ref):
    qi = pl.program_id(1)

    @pl.when(qi >= N_REAL_Q_TILES)
    def _():
        ya_ref[...] = jnp.zeros(ya_ref.shape, ya_ref.dtype)
        yb_ref[...] = jnp.zeros(yb_ref.shape, yb_ref.dtype)

    @pl.when(qi < N_REAL_Q_TILES)
    def _():
        n_it = (qi + 2) // 2
        rows_m = H_B * TQ
        lane = lax.broadcasted_iota(jnp.int32, (TQ, LANES), 1)
        heads = [slice(h * LANES, (h + 1) * LANES) for h in range(H_A)]
        q2 = []
        for hs in heads:
            qh = qa_ref[:, hs] * SCALE_A
            zero = jnp.zeros_like(qh)
            q2.append(jnp.concatenate([jnp.where(lane < D_HEAD_A, qh, zero),
                                       jnp.where(lane >= D_HEAD_A, qh, zero)], axis=0))
        q_lat = qlat_ref[...].reshape(rows_m, KV_LORA)
        group = lane >> 5
        q_pe = jnp.concatenate(
            [jnp.where(group == (head % 4), qpe_ref[:, (head // 4) * LANES:(head // 4 + 1) * LANES],
                       jnp.zeros((TQ, LANES), qpe_ref.dtype)) for head in range(H_B)], axis=0)

        md_ref[...] = jnp.full(md_ref.shape, NEG, F32)
        mm_ref[...] = jnp.full(mm_ref.shape, NEG, F32)

        def scores(j, masked):
            ks = pl.ds(pl.multiple_of(j * TK, TK), TK)
            off = qi * TQ - j * TK
            d0 = jnp.clip(qi - 2 * j, 0, 2)
            d1 = jnp.clip(qi - 2 * j - 1, 0, 2)
            if masked:
                row_d = lax.broadcasted_iota(jnp.int32, (2 * TQ, TK), 0) & (TQ - 1)
                lane_d = lax.broadcasted_iota(jnp.int32, (2 * TQ, TK), 1)
                keep_d = row_d + off >= lane_d
                row_m = lax.broadcasted_iota(jnp.int32, (rows_m, TK), 0) & (TQ - 1)
                lane_m = lax.broadcasted_iota(jnp.int32, (rows_m, TK), 1)
                keep_m = row_m + off >= lane_m
            for h, hs in enumerate(heads):
                b = jnp.concatenate([bias_ref[d0, h], bias_ref[d1, h]], axis=1)
                s = _nt_dot(q2[h], ka_ref[ks, hs]) * LOG2E + jnp.concatenate([b, b], axis=0)
                if masked:
                    s = jnp.where(keep_d, s, NEG)
                sd_ref[h, :, ks] = s
                md_ref[h] = jnp.maximum(md_ref[h], jnp.maximum(s[:, 0:LANES], s[:, LANES:TK]))
            s = (_nt_dot(q_lat, ckv_ref[ks, :]) + _nt_dot(q_pe, kpe_ref[ks, :])) * (SCALE_B * LOG2E)
            if masked:
                s = jnp.where(keep_m, s, NEG)
            sm_ref[:, ks] = s
            mm_ref[...] = jnp.maximum(mm_ref[...], jnp.maximum(s[:, 0:LANES], s[:, LANES:TK]))

        def scores_body(j, carry):
            scores(j, False)
            return carry

        lax.fori_loop(0, n_it - 1, scores_body, 0)
        scores(n_it - 1, True)

        m_d = [jnp.broadcast_to(jnp.max(md_ref[h], axis=-1, keepdims=True), (2 * TQ, LANES))
               for h in range(H_A)]
        m_m = jnp.broadcast_to(jnp.max(mm_ref[...], axis=-1, keepdims=True), (rows_m, LANES))
        accd_ref[...] = jnp.zeros(accd_ref.shape, F32)
        ld_ref[...] = jnp.zeros(ld_ref.shape, F32)
        accm_ref[...] = jnp.zeros(accm_ref.shape, F32)
        lm_ref[...] = jnp.zeros(lm_ref.shape, F32)

        def values_body(j, carry):
            ks = pl.ds(pl.multiple_of(j * TK, TK), TK)
            for h, hs in enumerate(heads):
                p = jnp.exp2(sd_ref[h, :, ks] - jnp.concatenate([m_d[h], m_d[h]], axis=1))
                ld_ref[h] = ld_ref[h] + (p[:, 0:LANES] + p[:, LANES:TK])
                accd_ref[h] = accd_ref[h] + _dot(p.astype(BF16), va_ref[ks, hs])
            p = jnp.exp2(sm_ref[:, ks] - jnp.concatenate([m_m, m_m], axis=1))
            lm_ref[...] = lm_ref[...] + (p[:, 0:LANES] + p[:, LANES:TK])
            accm_ref[...] = accm_ref[...] + _dot(p.astype(BF16), ckv_ref[ks, :])
            return carry

        lax.fori_loop(0, n_it, values_body, 0)

        lam = _diff_lambda(lam_ref)
        subln_w = subln_ref[...]
        for h, hs in enumerate(heads):
            o = accd_ref[h] / jnp.sum(ld_ref[h], axis=-1, keepdims=True)
            ya_ref[:, hs] = _diff_head_finish(o[0:TQ], o[TQ:2 * TQ], lam, subln_w).astype(ya_ref.dtype)
        ob = accm_ref[...] / jnp.sum(lm_ref[...], axis=-1, keepdims=True)
        for pair in range(H_B // 2):
            e0 = 2 * pair * TQ
            yb_ref[:, pair * LANES:(pair + 1) * LANES] = _mla_pair_finish(
                ob[e0:e0 + TQ], ob[e0 + TQ:e0 + 2 * TQ], wuv_ref, pair).astype(yb_ref.dtype)


def _prompt_attention(qa, qlat, qpe, kab, vab, ckvb, kpeb, bias_tiles, lam_p, subln_w, wuv):
    nb = qa.shape[0]
    qrows = lambda width: pl.BlockSpec((None, TQ, width), lambda b, i: (b, i, 0))
    keys = lambda width: pl.BlockSpec((None, T_PAD, width), lambda b, i: (b, 0, 0))
    full = lambda a: pl.BlockSpec(a.shape, lambda b, i: (0,) * a.ndim)
    return pl.pallas_call(
        _prompt_attn_kernel,
        grid=(nb, N_Q_TILES),
        in_specs=[qrows(W_A),
                  pl.BlockSpec((H_B, None, TQ, KV_LORA), lambda b, i: (0, b, i, 0)),
                  qrows(Q_ROPE_COLS),
                  keys(W_A), keys(W_A), keys(KV_LORA), keys(LANES),
                  full(bias_tiles), full(lam_p), full(subln_w), full(wuv)],
        out_specs=(qrows(H_A * D_V_A), qrows(H_B * D_V_B)),
        out_shape=(jax.ShapeDtypeStruct((nb, T_PAD, H_A * D_V_A), BF16),
                   jax.ShapeDtypeStruct((nb, T_PAD, H_B * D_V_B), BF16)),
        scratch_shapes=[pltpu.VMEM((H_A, 2 * TQ, T_PAD), F32),
                        pltpu.VMEM((H_B * TQ, T_PAD), F32),
                        pltpu.VMEM((H_A, 2 * TQ, LANES), F32),
                        pltpu.VMEM((H_B * TQ, LANES), F32),
                        pltpu.VMEM((H_A, 2 * TQ, D_V_A), F32),
                        pltpu.VMEM((H_A, 2 * TQ, LANES), F32),
                        pltpu.VMEM((H_B * TQ, KV_LORA), F32),
                        pltpu.VMEM((H_B * TQ, LANES), F32)],
        compiler_params=pltpu.CompilerParams(
            dimension_semantics=("parallel", "parallel"), vmem_limit_bytes=VMEM_LIMIT),
        name="prompt_attn",
    )(qa, qlat, qpe, kab, vab, ckvb, kpeb, bias_tiles, lam_p, subln_w, wuv)


ROWS_PER_PAGE = PAGE_SIZE * H_A


def _online_update(s, m_ref, l_ref):
    m = m_ref[...]
    m_new = jnp.maximum(m, jnp.max(s, axis=-1, keepdims=True))
    p = jnp.exp(s - m_new)
    corr = jnp.exp(m - m_new)
    l_ref[...] = l_ref[...] * corr + jnp.sum(p, axis=-1, keepdims=True)
    m_ref[...] = m_new
    return p, corr


def _paged_attn_kernel(pt_ref, qd_ref, ql_ref, qr_ref, bias_ref, bself_ref,
                       knew_ref, vnew_ref, cnew_ref, rnew_ref, ck_hbm, cv_hbm, cc_hbm, cr_hbm,
                       oa_ref, ob_ref, kbuf, vbuf, cbuf, rbuf, sem,
                       md_ref, ld_ref, mm_ref, lm_ref, acca_ref, accb_ref):
    g = PAGES_PER_STEP
    step = pl.program_id(1)
    n_steps = pl.num_programs(1)
    t = pl.program_id(0) * n_steps + step
    total = pl.num_programs(0) * n_steps
    slot = t & 1

    def page_copies(t_src, dst):
        copies = []
        for j in range(g):
            page = pt_ref[t_src * g + j]
            rows = pl.ds(pl.multiple_of(page * ROWS_PER_PAGE, ROWS_PER_PAGE), ROWS_PER_PAGE)
            copies += [pltpu.make_async_copy(ck_hbm.at[rows], kbuf.at[dst, j], sem.at[dst]),
                       pltpu.make_async_copy(cv_hbm.at[rows], vbuf.at[dst, j], sem.at[dst]),
                       pltpu.make_async_copy(cc_hbm.at[page], cbuf.at[dst, j], sem.at[dst]),
                       pltpu.make_async_copy(cr_hbm.at[page], rbuf.at[dst, j], sem.at[dst])]
        return copies

    def start_all(copies):
        for i, copy in enumerate(copies):
            copy.start(priority=(i // 4) % 2)

    @pl.when(t == 0)
    def _():
        start_all(page_copies(0, 0))

    @pl.when(t + 1 < total)
    def _():
        start_all(page_copies(t + 1, 1 - slot))

    @pl.when(step == 0)
    def _():
        md_ref[...] = jnp.full(md_ref.shape, NEG, F32)
        mm_ref[...] = jnp.full(mm_ref.shape, NEG, F32)
        ld_ref[...] = jnp.zeros(ld_ref.shape, F32)
        lm_ref[...] = jnp.zeros(lm_ref.shape, F32)
        acca_ref[...] = jnp.zeros(acca_ref.shape, F32)
        accb_ref[...] = jnp.zeros(accb_ref.shape, F32)

    qd = qd_ref[...]
    ql = ql_ref[...]
    qr = qr_ref[...]
    wd = g * ROWS_PER_PAGE
    bias = bias_ref[:, pl.ds(pl.multiple_of(step * wd, wd), wd)]

    for copy in page_copies(t, slot):
        copy.wait()

    sd = jnp.concatenate([_nt_dot(qd, kbuf[slot, j]) for j in range(g)], axis=1)
    sd = sd * SCALE_A + bias
    sm = jnp.concatenate([_nt_dot(ql, cbuf[slot, j]) + _dot(qr, rbuf[slot, j])
                          for j in range(g)], axis=1) * SCALE_B
    pd, corr_d = _online_update(sd, md_ref, ld_ref)
    pm, corr_m = _online_update(sm, mm_ref, lm_ref)
    pa = _dot(pd[:, 0:ROWS_PER_PAGE], vbuf[slot, 0])
    pb = _dot(pm[:, 0:PAGE_SIZE], cbuf[slot, 0])
    for j in range(1, g):
        pa = pa + _dot(pd[:, j * ROWS_PER_PAGE:(j + 1) * ROWS_PER_PAGE], vbuf[slot, j])
        pb = pb + _dot(pm[:, j * PAGE_SIZE:(j + 1) * PAGE_SIZE], cbuf[slot, j])
    acca_ref[...] = acca_ref[...] * corr_d + pa
    accb_ref[...] = accb_ref[...] * corr_m + pb

    @pl.when(step == pl.num_programs(1) - 1)
    def _():
        s_d = jnp.sum(qd * knew_ref[...], axis=-1, keepdims=True) * SCALE_A + bself_ref[...]
        s_m = (jnp.sum(ql * cnew_ref[...], axis=-1, keepdims=True)
               + jnp.sum(qr * rnew_ref[...], axis=-1, keepdims=True)) * SCALE_B
        p_d, c_d = _online_update(s_d, md_ref, ld_ref)
        p_m, c_m = _online_update(s_m, mm_ref, lm_ref)
        oa_ref[...] = (acca_ref[...] * c_d + p_d * vnew_ref[...]) / ld_ref[...]
        ob_ref[...] = (accb_ref[...] * c_m + p_m * cnew_ref[...]) / lm_ref[...]


def _paged_attention(page_table, qd, ql, qr, bias, bias_self, k_new, v_new, c_new, r_new,
                     cache_k, cache_v, cache_c, cache_r):
    g = PAGES_PER_STEP
    n_steps = N_PAGES // g
    half = H_A * 2
    per_seq = lambda a: pl.BlockSpec((None,) + a.shape[1:], lambda b, s, pt: (b,) + (0,) * (a.ndim - 1))
    full = lambda a: pl.BlockSpec(a.shape, lambda b, s, pt: (0,) * a.ndim)
    hbm = pl.BlockSpec(memory_space=pl.ANY)
    grid_spec = pltpu.PrefetchScalarGridSpec(
        num_scalar_prefetch=1,
        grid=(DEC_BATCH, n_steps),
        in_specs=[per_seq(qd), per_seq(ql), per_seq(qr), full(bias), full(bias_self),
                  per_seq(k_new), per_seq(v_new), per_seq(c_new), per_seq(r_new),
                  hbm, hbm, hbm, hbm],
        out_specs=(pl.BlockSpec((None, half, D_V_A), lambda b, s, pt: (b, 0, 0)),
                   pl.BlockSpec((None, H_B, KV_LORA), lambda b, s, pt: (b, 0, 0))),
        scratch_shapes=[pltpu.VMEM((2, g, ROWS_PER_PAGE, 2 * D_HEAD_A), F32),
                        pltpu.VMEM((2, g, ROWS_PER_PAGE, D_V_A), F32),
                        pltpu.VMEM((2, g, PAGE_SIZE, KV_LORA), F32),
                        pltpu.VMEM((2, g, D_ROPE, PAGE_SIZE), F32),
                        pltpu.SemaphoreType.DMA((2,)),
                        pltpu.VMEM((half, 1), F32), pltpu.VMEM((half, 1), F32),
                        pltpu.VMEM((H_B, 1), F32), pltpu.VMEM((H_B, 1), F32),
                        pltpu.VMEM((half, D_V_A), F32), pltpu.VMEM((H_B, KV_LORA), F32)],
    )
    return pl.pallas_call(
        _paged_attn_kernel,
        grid_spec=grid_spec,
        out_shape=(jax.ShapeDtypeStruct((DEC_BATCH, half, D_V_A), F32),
                   jax.ShapeDtypeStruct((DEC_BATCH, H_B, KV_LORA), F32)),
        compiler_params=pltpu.CompilerParams(
            dimension_semantics=("arbitrary", "arbitrary"), vmem_limit_bytes=VMEM_LIMIT),
        name="paged_attn",
    )(page_table, qd, ql, qr, bias, bias_self, k_new, v_new, c_new, r_new,
      cache_k, cache_v, cache_c, cache_r)


def _finish_kernel(oa_ref, ob_ref, lam_ref, subln_ref, wuv_ref, ya_ref, yb_ref):
    lam = _diff_lambda(lam_ref)
    subln_w = subln_ref[...]
    for head in range(H_A):
        ya_ref[:, head * D_V_A:(head + 1) * D_V_A] = _diff_head_finish(
            oa_ref[2 * head], oa_ref[2 * head + 1], lam, subln_w).astype(ya_ref.dtype)
    for pair in range(H_B // 2):
        yb_ref[:, pair * LANES:(pair + 1) * LANES] = _mla_pair_finish(
            ob_ref[2 * pair], ob_ref[2 * pair + 1], wuv_ref, pair).astype(yb_ref.dtype)


def _finish(oa_t, ob_t, lam_p, subln_w, wuv):
    n = oa_t.shape[1]
    full = lambda a: pl.BlockSpec(a.shape, lambda: (0,) * a.ndim)
    return pl.pallas_call(
        _finish_kernel,
        in_specs=[full(oa_t), full(ob_t), full(lam_p), full(subln_w), full(wuv)],
        out_specs=(pl.BlockSpec((n, H_A * D_V_A), lambda: (0, 0)),
                   pl.BlockSpec((n, H_B * D_V_B), lambda: (0, 0))),
        out_shape=(jax.ShapeDtypeStruct((n, H_A * D_V_A), BF16),
                   jax.ShapeDtypeStruct((n, H_B * D_V_B), BF16)),
        name="sample_finish",
    )(oa_t, ob_t, lam_p, subln_w, wuv)


def _ffn_kernel(sample, *refs):
    if sample:
        (x_ref, ya_ref, yb_ref, s0_ref, s1_ref, nw_pre, wgate, woa, wob, wo, nw_post, nw_fpre, wup,
         convw, convb, wdown, nw_fpost, y_ref, aux_ref) = refs
    else:
        (x_ref, ya_ref, yb_ref, nw_pre, wgate, woa, wob, wo, nw_post, nw_fpre, wup,
         convw, convb, wdown, nw_fpost, y_ref, aux_ref, carry_ref) = refs
    x = x_ref[...]
    tm = x.shape[0]
    n = _rms(x, nw_pre[...]).astype(BF16)
    gate = jax.nn.sigmoid(_dot(n, wgate[...]))
    merged = (gate[:, 0:D_MODEL] * _dot(ya_ref[...], woa[...])
              + gate[:, D_MODEL:2 * D_MODEL] * _dot(yb_ref[...], wob[...]))
    mix = _dot(merged.astype(BF16), wo[...])
    h = x + _rms(mix, nw_post[...])
    n2 = _rms(h, nw_fpre[...]).astype(BF16)
    up = _dot(n2, wup[...])
    a = up[:, 0:D_FF]
    b = up[:, D_FF:2 * D_FF]
    cw = convw[...]
    if sample:
        prev2 = s0_ref[...]
        prev1 = s1_ref[...]
        aux_ref[...] = a
    else:
        ti = pl.program_id(1)

        @pl.when(ti == 0)
        def _():
            carry_ref[...] = jnp.zeros(carry_ref.shape, F32)

        row = lax.broadcasted_iota(jnp.int32, (tm, 1), 0)
        pos = ti * tm + row
        carry = carry_ref[...]
        prev1 = jnp.where(row == 0, carry[7:8], pltpu.roll(a, 1, 0))
        prev2 = jnp.where(row == 0, carry[6:7], jnp.where(row == 1, carry[7:8], pltpu.roll(a, 2, 0)))
        prev1 = jnp.where(pos >= 1, prev1, 0.0)
        prev2 = jnp.where(pos >= 2, prev2, 0.0)
        carry_ref[...] = a[tm - 8:tm]
        last = T_P - 1 - (T_PAD - tm)

        @pl.when(ti == pl.num_programs(1) - 1)
        def _():
            aux_ref[...] = a[last - 1:last + 1]

    conv = convb[...] + cw[0:1] * prev2
    conv = conv + cw[1:2] * prev1
    conv = conv + cw[2:3] * a
    hh = jax.nn.gelu(conv) * b
    f = _dot(hh.astype(BF16), wdown[...])
    y_ref[...] = h + _rms(f, nw_fpost[...])


def _ffn(x3, ya3, yb3, wts, tm, state=None):
    nb, t, _ = x3.shape
    nt = t // tm
    sample = state is not None
    rows = lambda width: pl.BlockSpec((None, tm, width), lambda b, i: (b, i, 0))
    full = lambda a: pl.BlockSpec(a.shape, lambda b, i: (0,) * a.ndim,
                                  pipeline_mode=pl.Buffered(1))
    weights = (wts['norm_mix_pre'], wts['w_gate'], wts['w_oa'], wts['w_ob'], wts['w_o'],
               wts['norm_mix_post'], wts['norm_ffn_pre'], wts['w_up'], wts['conv_w'], wts['conv_b'],
               wts['w_down'], wts['norm_ffn_post'])
    acts = [x3, ya3, yb3]
    act_specs = [rows(D_MODEL), rows(H_A * D_V_A), rows(H_B * D_V_B)]
    if sample:
        acts += [state[0], state[1]]
        act_specs += [rows(D_FF), rows(D_FF)]
        aux_shape = jax.ShapeDtypeStruct((nb, t, D_FF), F32)
        aux_spec = rows(D_FF)
        scratch = []
    else:
        aux_shape = jax.ShapeDtypeStruct((nb, 2, D_FF), F32)
        aux_spec = pl.BlockSpec((None, 2, D_FF), lambda b, i: (b, 0, 0))
        scratch = [pltpu.VMEM((8, D_FF), F32)]
    return pl.pallas_call(
        functools.partial(_ffn_kernel, sample),
        grid=(nb, nt),
        in_specs=act_specs + [full(w) for w in weights],
        out_specs=(rows(D_MODEL), aux_spec),
        out_shape=(jax.ShapeDtypeStruct((nb, t, D_MODEL), F32), aux_shape),
        scratch_shapes=scratch,
        compiler_params=pltpu.CompilerParams(
            dimension_semantics=("parallel", "arbitrary"), vmem_limit_bytes=VMEM_LIMIT),
        name="ffn_sample" if sample else "ffn_prompt",
    )(*acts, *weights)


def _rope_tables(pos):
    half = D_ROPE // 2
    freqs = ROPE_BASE ** (-jnp.arange(half, dtype=F32) / half)
    ang = pos.astype(F32)[:, None] * freqs
    cos, sin = jnp.cos(ang), jnp.sin(ang)
    return (jnp.tile(jnp.concatenate([cos, cos], axis=-1), (1, H_B)),
            jnp.tile(jnp.concatenate([-sin, sin], axis=-1), (1, H_B)))


def _prepare_weights(norm_mix_pre, norm_mix_post, norm_ffn_pre, norm_ffn_post, w_in, w_gate, q_norm_w,
                     kv_norm_w, w_uq, w_ukv, w_oa, w_ob, w_o, w_up, conv_w, conv_b, w_down):
    row = lambda v: v[0].reshape(1, -1)
    w_in0 = w_in[0]
    w_in_aug = jnp.concatenate([w_in0] + [w_in0[:, IN_KPE:]] * 3, axis=1).astype(BF16)
    w_uq3 = w_uq[0].reshape(Q_LORA, H_B, D_NOPE + D_ROPE)
    w_uq_perm = jnp.concatenate([w_uq3[:, :, :D_NOPE].reshape(Q_LORA, Q_NOPE_COLS),
                                 w_uq3[:, :, D_NOPE:].reshape(Q_LORA, Q_ROPE_COLS)], axis=1).astype(BF16)
    w_ukv0 = w_ukv[0]
    uk = jnp.transpose(w_ukv0[:, :, :D_NOPE], (1, 2, 0))
    uv = jnp.transpose(w_ukv0[:, :, D_NOPE:], (1, 0, 2))
    zk = jnp.zeros_like(uk)
    zv = jnp.zeros_like(uv)
    even = (jnp.arange(H_B) % 2 == 0)
    w_uk = jnp.where(even[:, None, None], jnp.concatenate([uk, zk], axis=1),
                     jnp.concatenate([zk, uk], axis=1)).astype(BF16)
    w_uv = jnp.where(even[:, None, None], jnp.concatenate([uv, zv], axis=2),
                     jnp.concatenate([zv, uv], axis=2)).astype(BF16)
    return dict(
        norm_mix_pre=row(norm_mix_pre), norm_mix_post=row(norm_mix_post),
        norm_ffn_pre=row(norm_ffn_pre), norm_ffn_post=row(norm_ffn_post),
        q_norm_w=row(q_norm_w), kv_norm_w=row(kv_norm_w),
        w_in=w_in_aug, w_uq=w_uq_perm, w_uk=w_uk, w_uv=w_uv,
        w_gate=w_gate[0].astype(BF16), w_oa=w_oa[0].astype(BF16), w_ob=w_ob[0].astype(BF16),
        w_o=w_o[0].astype(BF16), w_up=w_up[0].astype(BF16), w_down=w_down[0].astype(BF16),
        conv_w=conv_w[0], conv_b=row(conv_b))


def kernel(x_prompt, x_sample, cache_diff_k, cache_diff_v, cache_mla_latent, cache_mla_rope, state_conv, page_table, meta_tokens, rel_bias, norm_mix_pre, norm_mix_post, norm_ffn_pre, norm_ffn_post, w_in, w_gate, lambda_q1, lambda_k1, lambda_q2, lambda_k2, subln_w, q_norm_w, kv_norm_w, w_uq, w_ukv, w_oa, w_ob, w_o, w_up, conv_w, conv_b, w_down):
    assert x_prompt.shape == (BATCH, SEQ, D_MODEL) and x_sample.shape == (DEC_BATCH, 1, D_MODEL)
    assert page_table.shape == (DEC_BATCH, N_PAGES) and w_in.shape[0] == 1
    wts = _prepare_weights(norm_mix_pre, norm_mix_post, norm_ffn_pre, norm_ffn_post, w_in, w_gate,
                           q_norm_w, kv_norm_w, w_uq, w_ukv, w_oa, w_ob, w_o, w_up, conv_w, conv_b,
                           w_down)
    lam_p = jnp.concatenate([lambda_q1, lambda_k1, lambda_q2, lambda_k2], axis=0)
    subln = subln_w[0].reshape(1, D_V_A)

    i = np.arange(TQ)
    tile_dist = np.stack([np.maximum(d * TQ + i[:, None] - i[None, :], 0) for d in range(3)])
    bias_tiles = _bias_lookup(rel_bias, jnp.asarray(_t5_bucket(tile_dist).reshape(3 * TQ, TQ)))
    bias_tiles = bias_tiles.reshape(H_A, 3, TQ, TQ).transpose(1, 0, 2, 3) * LOG2E
    past_dist = PAST_LEN - np.arange(PAST_LEN + PAGE_SIZE)
    past_dist = np.where(past_dist >= 0, past_dist, 0)
    past_bias = _bias_lookup(rel_bias, jnp.asarray(_t5_bucket(past_dist).reshape(N_PAGES + 1, PAGE_SIZE)))
    past_bias = past_bias.reshape(H_A, PAST_LEN + PAGE_SIZE)
    same_head = jnp.asarray(np.eye(H_A, dtype=bool))
    bias_past = jnp.where(same_head[:, None, :], past_bias[:, :PAST_LEN, None], NEG)
    bias_past = jnp.repeat(bias_past.reshape(H_A, PAST_LEN * H_A), 2, axis=0)
    bias_self = jnp.repeat(past_bias[:, PAST_LEN:PAST_LEN + 1], 2, axis=0)

    meta = jnp.broadcast_to(meta_tokens.astype(x_prompt.dtype)[None], (BATCH, N_META, D_MODEL))
    hp = jnp.concatenate([meta, x_prompt, jnp.zeros((BATCH, T_PAD - T_P, D_MODEL), x_prompt.dtype)],
                         axis=1)
    cos_p, sin_p = _rope_tables(jnp.arange(T_PAD, dtype=jnp.int32))
    (qa, qlat, qpe, ka, va, ckv, kpe, kab, vab, ckvb, kpeb) = _project(
        hp, cos_p, sin_p, wts, TM_DENSE, BF16)
    ya, yb = _prompt_attention(qa, qlat, qpe, kab, vab, ckvb, kpeb, bias_tiles, lam_p, subln,
                               wts['w_uv'])
    y_p, conv_p = _ffn(hp, ya, yb, wts, TM_DENSE)

    xs = x_sample.reshape(1, DEC_BATCH, D_MODEL)
    cos_s, sin_s = _rope_tables(jnp.full((DEC_BATCH,), PAST_LEN, jnp.int32))
    (qa_s, qlat_s, qpe_s, ka_s, va_s, ckv_s, kpe_s, _, _, _, _) = _project(
        xs, cos_s, sin_s, wts, DEC_BATCH, F32)
    lane_map = np.arange(2 * D_HEAD_A)[None, :] // D_HEAD_A == (np.arange(2 * H_A) % 2)[:, None]
    per_head_rows = lambda a: jnp.repeat(a.reshape(DEC_BATCH, H_A, 2 * D_HEAD_A), 2, axis=1)
    qd = jnp.where(jnp.asarray(lane_map)[None], per_head_rows(qa_s[0]), 0.0)
    ql = jnp.transpose(qlat_s[:, 0], (1, 0, 2))
    qr = qpe_s[0].reshape(DEC_BATCH, H_B, D_ROPE)
    n_pool = cache_diff_k.shape[1]
    oa_s, ob_s = _paged_attention(
        page_table.reshape(-1), qd, ql, qr, bias_past, bias_self,
        per_head_rows(ka_s[0]), per_head_rows(va_s[0]),
        ckv_s.reshape(DEC_BATCH, 1, KV_LORA), kpe_s.reshape(DEC_BATCH, 1, D_ROPE),
        cache_diff_k.reshape(n_pool * ROWS_PER_PAGE, 2 * D_HEAD_A),
        cache_diff_v.reshape(n_pool * ROWS_PER_PAGE, D_V_A),
        cache_mla_latent[0], jnp.swapaxes(cache_mla_rope[0], 1, 2))
    ya_s, yb_s = _finish(jnp.transpose(oa_s, (1, 0, 2)), jnp.transpose(ob_s, (1, 0, 2)), lam_p, subln,
                         wts['w_uv'])
    state = (state_conv[0, :, 0].reshape(1, DEC_BATCH, D_FF), state_conv[0, :, 1].reshape(1, DEC_BATCH, D_FF))
    y_s, a_s = _ffn(xs, ya_s.reshape(1, DEC_BATCH, -1), yb_s.reshape(1, DEC_BATCH, -1), wts, DEC_BATCH,
                    state=state)

    shape5 = lambda a, n, t: a.reshape(1, n, t, H_A, 2 * D_HEAD_A)
    return (
        y_p[:, N_META:T_P],
        y_s.reshape(DEC_BATCH, 1, D_MODEL),
        shape5(ka[:, :T_P], BATCH, T_P),
        shape5(va[:, :T_P], BATCH, T_P),
        ckv[:, :T_P][None],
        kpe[:, :T_P][None],
        conv_p[None],
        shape5(ka_s, DEC_BATCH, 1),
        shape5(va_s, DEC_BATCH, 1),
        ckv_s.reshape(1, DEC_BATCH, 1, KV_LORA),
        kpe_s.reshape(1, DEC_BATCH, 1, D_ROPE),
        jnp.stack([state_conv[0, :, 1], a_s[0]], axis=1)[None],
    )
```
